```python
import math
import jax, jax.numpy as jnp
from jax import lax
import numpy as np

D_MODEL = 1024
BATCH = 4
SEQ = 8192
DEPTH = 2

D_MIX = D_MODEL
D_LRU = 3 * D_MODEL // 8
D_RET = 3 * D_MODEL // 8
D_SSM = D_MIX - D_LRU - D_RET
LRU_HEADS = 6
LRU_BLOCK = D_LRU // LRU_HEADS
CONV_WIDTH = 4
LRU_C = 8.0
RET_HEADS = 6
RET_HEAD_DIM = D_RET // RET_HEADS
RET_CHUNK = 128
ROPE_BASE = 10000.0
SSM_GROUP = 16
SSM_GROUPS = D_SSM // SSM_GROUP
SSM_STATE = 64
D_FF = ((8 * D_MODEL // 3 + 255) // 256) * 256
N_IN = 2 * D_LRU + 4 * D_RET + D_SSM
SPLITS = (D_LRU, 2 * D_LRU, 2 * D_LRU + D_RET, 2 * D_LRU + 2 * D_RET,
          2 * D_LRU + 3 * D_RET, 2 * D_LRU + 4 * D_RET)
N_MOD = 9
DEEPNORM_ALPHA = (2.0 * DEPTH) ** 0.25
DEEPNORM_BETA = (8.0 * DEPTH) ** -0.25
MACARON_HALF = 0.5
LN_EPS = 1e-5

kernel_name = "hymba_style_lru_retention_s5_macaron_deepnorm"

F32 = jnp.float32


def layer_norm(x, g, b):
    xf = x.astype(F32)
    mu = jnp.mean(xf, -1, keepdims=True)
    var = jnp.mean(jnp.square(xf - mu), -1, keepdims=True)
    return ((xf - mu) * lax.rsqrt(var + LN_EPS) * g.astype(F32) + b.astype(F32)).astype(x.dtype)


def modulate(x, shift, scale):
    return x * (1.0 + scale[:, None, :]) + shift[:, None, :]


def swiglu(h, w1, w3, w2):
    return (jax.nn.silu(h @ w1) * (h @ w3)) @ w2


def causal_conv(u, w, b):
    S = u.shape[1]
    up = jnp.pad(u, ((0, 0), (CONV_WIDTH - 1, 0), (0, 0)))
    out = b
    for k in range(CONV_WIDTH):
        out = out + up[:, k:k + S, :] * w[k]
    return out


def _linear_combine(e1, e2):
    a1, b1 = e1
    a2, b2 = e2
    return a1 * a2, a2 * b1 + b2


def rglru(u, w_a, b_a, w_x, b_x, lam):
    Bsz, S, _ = u.shape
    uh = u.reshape(Bsz, S, LRU_HEADS, LRU_BLOCK)
    r = jax.nn.sigmoid(jnp.einsum('bshi,hij->bshj', uh, w_a).reshape(Bsz, S, D_LRU) + b_a).astype(F32)
    i = jax.nn.sigmoid(jnp.einsum('bshi,hij->bshj', uh, w_x).reshape(Bsz, S, D_LRU) + b_x).astype(F32)
    log_a = -LRU_C * r * jax.nn.softplus(-lam.astype(F32))
    a = jnp.exp(log_a)
    bterm = jnp.sqrt(-jnp.expm1(2.0 * log_a)) * (i * u.astype(F32))
    _, h = lax.associative_scan(_linear_combine, (a, bterm), axis=1)
    return h.astype(u.dtype)


def rotary(t, pos):
    half = RET_HEAD_DIM // 2
    inv = ROPE_BASE ** (-jnp.arange(half, dtype=F32) / half)
    ang = pos.astype(F32)[..., None] * inv
    cos = jnp.cos(ang)[:, :, None, :]
    sin = jnp.sin(ang)[:, :, None, :]
    tf = t.astype(F32)
    t1, t2 = tf[..., :half], tf[..., half:]
    return jnp.concatenate([t1 * cos - t2 * sin, t2 * cos + t1 * sin], -1)


def retention(q, k, v, g, pos, gn_g, gn_b):
    Bsz, S, _ = q.shape
    H, Dh, C = RET_HEADS, RET_HEAD_DIM, RET_CHUNK
    NC = S // C
    qr = rotary(q.reshape(Bsz, S, H, Dh), pos)
    kr = rotary(k.reshape(Bsz, S, H, Dh), pos) * (Dh ** -0.5)
    vr = v.astype(F32).reshape(Bsz, S, H, Dh)

    def chunks(t):
        return t.reshape(Bsz, NC, C, H, Dh).transpose(0, 3, 1, 2, 4)

    qc, kc, vc = chunks(qr), chunks(kr), chunks(vr)
    log_gamma = jnp.log1p(-jnp.exp2(-5.0 - jnp.arange(H, dtype=F32)))
    idx = jnp.arange(C, dtype=F32)
    diff = idx[:, None] - idx[None, :]
    causal = diff >= 0
    decay_mask = jnp.where(causal, jnp.exp(log_gamma[:, None, None] * jnp.where(causal, diff, 0.0)), 0.0)
    scores = jnp.einsum('bhnqd,bhnkd->bhnqk', qc, kc) * decay_mask[:, None]
    o_inner = jnp.einsum('bhnqk,bhnkd->bhnqd', scores, vc)
    k_decay = jnp.exp(log_gamma[:, None] * (C - 1 - idx))
    kv = jnp.einsum('bhnkd,bhnke->bhnde', kc * k_decay[:, None, :, None], vc)
    chunk_decay = jnp.exp(log_gamma * C)[None, :, None, None]

    def step(state, kv_n):
        return chunk_decay * state + kv_n, state

    init = jnp.zeros((Bsz, H, Dh, Dh), F32)
    _, prev = lax.scan(step, init, kv.transpose(2, 0, 1, 3, 4))
    prev = prev.transpose(1, 2, 0, 3, 4)
    q_decay = jnp.exp(log_gamma[:, None] * (idx + 1.0))
    o_cross = jnp.einsum('bhnqd,bhnde->bhnqe', qc * q_decay[:, None, :, None], prev)
    o = o_inner + o_cross
    mu = jnp.mean(o, -1, keepdims=True)
    var = jnp.mean(jnp.square(o - mu), -1, keepdims=True)
    o = (o - mu) * lax.rsqrt(var + LN_EPS)
    o = o.transpose(0, 2, 3, 1, 4).reshape(Bsz, S, D_RET) * gn_g.astype(F32) + gn_b.astype(F32)
    return (jax.nn.silu(g.astype(F32)) * o).astype(q.dtype)


def _complex_combine(e1, e2):
    ar1, ai1, br1, bi1 = e1
    ar2, ai2, br2, bi2 = e2
    return (ar2 * ar1 - ai2 * ai1, ar2 * ai1 + ai2 * ar1,
            ar2 * br1 - ai2 * bi1 + br2, ar2 * bi1 + ai2 * br1 + bi2)


def s5(u, lam_re, lam_im, log_step, b_re, b_im, c_re, c_im, d_skip, w_glu, b_glu):
    Bsz, S, _ = u.shape
    uf = u.astype(F32).reshape(Bsz, S, SSM_GROUPS, SSM_GROUP)
    lr, li = lam_re.astype(F32), lam_im.astype(F32)
    dt = jnp.exp(log_step.astype(F32))[:, None]
    mag = jnp.exp(lr * dt)
    zr, zi = mag * jnp.cos(li * dt), mag * jnp.sin(li * dt)
    den = lr * lr + li * li
    er = ((zr - 1.0) * lr + zi * li) / den
    ei = (zi * lr - (zr - 1.0) * li) / den
    br, bi = b_re.astype(F32), b_im.astype(F32)
    bbar_re = er[..., None] * br - ei[..., None] * bi
    bbar_im = er[..., None] * bi + ei[..., None] * br
    bu_re = jnp.einsum('bsgh,gph->bsgp', uf, bbar_re)
    bu_im = jnp.einsum('bsgh,gph->bsgp', uf, bbar_im)
    a_re = jnp.broadcast_to(zr, (1, S, SSM_GROUPS, SSM_STATE))
    a_im = jnp.broadcast_to(zi, (1, S, SSM_GROUPS, SSM_STATE))
    _, _, xr, xi = lax.associative_scan(_complex_combine, (a_re, a_im, bu_re, bu_im), axis=1)
    y = (jnp.einsum('bsgp,ghp->bsgh', xr, c_re.astype(F32))
         - jnp.einsum('bsgp,ghp->bsgh', xi, c_im.astype(F32)))
    y = y.reshape(Bsz, S, D_SSM) + d_skip.astype(F32) * u.astype(F32)
    y = jax.nn.gelu(y)
    y = y * jax.nn.sigmoid(y @ w_glu.astype(F32) + b_glu.astype(F32))
    return y.astype(u.dtype)


def token_mixer(h, pos, w_in, conv_w, conv_b, lru_wa, lru_ba, lru_wx, lru_bx, lru_lam,
                ret_gn_g, ret_gn_b, ssm_lam_re, ssm_lam_im, ssm_log_step, ssm_b_re, ssm_b_im,
                ssm_c_re, ssm_c_im, ssm_d, ssm_w_glu, ssm_b_glu, w_out):
    z = h @ w_in
    u_lru, g_lru, q, k, v, g_ret, u_ssm = jnp.split(z, SPLITS, axis=-1)
    y_lru = rglru(causal_conv(u_lru, conv_w, conv_b), lru_wa, lru_ba, lru_wx, lru_bx, lru_lam) * jax.nn.gelu(g_lru)
    y_ret = retention(q, k, v, g_ret, pos, ret_gn_g, ret_gn_b)
    y_ssm = s5(u_ssm, ssm_lam_re, ssm_lam_im, ssm_log_step, ssm_b_re, ssm_b_im,
               ssm_c_re, ssm_c_im, ssm_d, ssm_w_glu, ssm_b_glu)
    return jnp.concatenate([y_lru, y_ret, y_ssm], axis=-1) @ w_out


def _normal(k, shape, scale):
    return jax.random.normal(k, shape, F32) * scale


def setup_inputs(seed: int = 0) -> dict:
    key = jax.random.key(seed)
    ks = iter(jax.random.split(key, 48))
    L, D, F = DEPTH, D_MODEL, D_FF
    G, P, Hs = SSM_GROUPS, SSM_STATE, SSM_GROUP
    x = _normal(next(ks), (BATCH, SEQ, D), 1.0)
    c = _normal(next(ks), (BATCH, D), 1.0)
    positions = jnp.broadcast_to(jnp.arange(SEQ, dtype=jnp.int32)[None, :], (BATCH, SEQ))
    u_lam = jax.random.uniform(next(ks), (L, D_LRU), F32, 0.9, 0.999)
    a0 = u_lam ** (1.0 / LRU_C)
    lru_lam = jnp.log(a0) - jnp.log1p(-a0)
    return {
        "x": x, "c": c, "positions": positions,
        "ada_w": _normal(next(ks), (L, D, N_MOD * D), 0.5 * D ** -0.5),
        "ada_b": _normal(next(ks), (L, N_MOD * D), 0.01),
        "ln_g": 1.0 + _normal(next(ks), (L, 3, D), 0.02),
        "ln_b": _normal(next(ks), (L, 3, D), 0.01),
        "ffn1_w1": _normal(next(ks), (L, D, F), D ** -0.5),
        "ffn1_w3": _normal(next(ks), (L, D, F), D ** -0.5),
        "ffn1_w2": _normal(next(ks), (L, F, D), DEEPNORM_BETA * F ** -0.5),
        "mix_w_in": _normal(next(ks), (L, D, N_IN), D ** -0.5),
        "conv_w": _normal(next(ks), (L, CONV_WIDTH, D_LRU), CONV_WIDTH ** -0.5),
        "conv_b": _normal(next(ks), (L, D_LRU), 0.01),
        "lru_wa": _normal(next(ks), (L, LRU_HEADS, LRU_BLOCK, LRU_BLOCK), LRU_BLOCK ** -0.5),
        "lru_ba": _normal(next(ks), (L, D_LRU), 0.01),
        "lru_wx": _normal(next(ks), (L, LRU_HEADS, LRU_BLOCK, LRU_BLOCK), LRU_BLOCK ** -0.5),
        "lru_bx": _normal(next(ks), (L, D_LRU), 0.01),
        "lru_lam": lru_lam,
        "ret_gn_g": 1.0 + _normal(next(ks), (L, D_RET), 0.02),
        "ret_gn_b": _normal(next(ks), (L, D_RET), 0.01),
        "ssm_lam_re": -0.5 + _normal(next(ks), (L, G, P), 0.005),
        "ssm_lam_im": math.pi * jnp.broadcast_to(jnp.arange(P, dtype=F32), (L, G, P)) + _normal(next(ks), (L, G, P), 0.005),
        "ssm_log_step": jax.random.uniform(next(ks), (L, G), F32, math.log(0.001), math.log(0.1)),
        "ssm_b_re": _normal(next(ks), (L, G, P, Hs), (2.0 * Hs) ** -0.5),
        "ssm_b_im": _normal(next(ks), (L, G, P, Hs), (2.0 * Hs) ** -0.5),
        "ssm_c_re": _normal(next(ks), (L, G, Hs, P), (2.0 * P) ** -0.5),
        "ssm_c_im": _normal(next(ks), (L, G, Hs, P), (2.0 * P) ** -0.5),
        "ssm_d": _normal(next(ks), (L, D_SSM), 1.0),
        "ssm_w_glu": _normal(next(ks), (L, D_SSM, D_SSM), D_SSM ** -0.5),
        "ssm_b_glu": _normal(next(ks), (L, D_SSM), 0.01),
        "mix_w_out": _normal(next(ks), (L, D_MIX, D), DEEPNORM_BETA * D_MIX ** -0.5),
        "ffn2_w1": _normal(next(ks), (L, D, F), D ** -0.5),
        "ffn2_w3": _normal(next(ks), (L, D, F), D ** -0.5),
        "ffn2_w2": _normal(next(ks), (L, F, D), DEEPNORM_BETA * F ** -0.5),
    }


def reference(x, c, positions, ada_w, ada_b, ln_g, ln_b, ffn1_w1, ffn1_w3, ffn1_w2,
              mix_w_in, conv_w, conv_b, lru_wa, lru_ba, lru_wx, lru_bx, lru_lam,
              ret_gn_g, ret_gn_b, ssm_lam_re, ssm_lam_im, ssm_log_step, ssm_b_re, ssm_b_im,
              ssm_c_re, ssm_c_im, ssm_d, ssm_w_glu, ssm_b_glu, mix_w_out,
              ffn2_w1, ffn2_w3, ffn2_w2):
    cond = jax.nn.silu(c)
    for l in range(DEPTH):
        mod = cond @ ada_w[l] + ada_b[l]
        sh1, sc1, gt1, sh2, sc2, gt2, sh3, sc3, gt3 = jnp.split(mod, N_MOD, axis=-1)
        f1 = swiglu(modulate(x, sh1, sc1), ffn1_w1[l], ffn1_w3[l], ffn1_w2[l])
        x = layer_norm(DEEPNORM_ALPHA * x + MACARON_HALF * gt1[:, None, :] * f1, ln_g[l, 0], ln_b[l, 0])
        m = token_mixer(modulate(x, sh2, sc2), positions, mix_w_in[l], conv_w[l], conv_b[l],
                        lru_wa[l], lru_ba[l], lru_wx[l], lru_bx[l], lru_lam[l],
                        ret_gn_g[l], ret_gn_b[l], ssm_lam_re[l], ssm_lam_im[l], ssm_log_step[l],
                        ssm_b_re[l], ssm_b_im[l], ssm_c_re[l], ssm_c_im[l], ssm_d[l],
                        ssm_w_glu[l], ssm_b_glu[l], mix_w_out[l])
        x = layer_norm(DEEPNORM_ALPHA * x + gt2[:, None, :] * m, ln_g[l, 1], ln_b[l, 1])
        f2 = swiglu(modulate(x, sh3, sc3), ffn2_w1[l], ffn2_w3[l], ffn2_w2[l])
        x = layer_norm(DEEPNORM_ALPHA * x + MACARON_HALF * gt3[:, None, :] * f2, ln_g[l, 2], ln_b[l, 2])
    return x
```

```python
import functools
import math

import numpy as np
import jax
import jax.numpy as jnp
from jax import lax
from jax.experimental import pallas as pl
from jax.experimental.pallas import tpu as pltpu

F32 = jnp.float32
BF16 = jnp.bfloat16

LRU_HEADS = 6
CONV_WIDTH = 4
LRU_C = 8.0
RET_HEADS = 6
RET_CHUNK = 128
ROPE_BASE = 10000.0
SSM_GROUP = 16
SSM_STATE = 64
N_MOD = 9
MACARON_HALF = 0.5
LN_EPS = 1e-5

LANES = 128
SUBLANES = 8
VMEM_LIMIT_BYTES = 56 * 1024 * 1024


def _silu(v):
    return v * jax.nn.sigmoid(v)


def _layer_norm_rows(v, g, b):
    mu = jnp.mean(v, axis=-1, keepdims=True)
    d = v - mu
    var = jnp.mean(d * d, axis=-1, keepdims=True)
    return d * lax.rsqrt(var + LN_EPS) * g + b


def _const_spec(shape):
    zeros = (0,) * len(shape)
    return pl.BlockSpec(shape, lambda *_: zeros, pipeline_mode=pl.Buffered(1))


def _ada_kernel(c_ref, w_ref, b_ref, o_ref):
    cond = _silu(c_ref[...]).astype(BF16)
    o_ref[0] = jnp.dot(cond, w_ref[0].astype(BF16), preferred_element_type=F32) + b_ref[0]


def _ada_call(c_pad, ada_w, ada_b):
    depth, d, n = ada_w.shape
    rows = c_pad.shape[0]
    tn = n // N_MOD
    return pl.pallas_call(
        _ada_kernel,
        grid=(depth, n // tn),
        in_specs=[
            pl.BlockSpec((rows, d), lambda l, j: (0, 0)),
            pl.BlockSpec((1, d, tn), lambda l, j: (l, 0, j)),
            pl.BlockSpec((1, 1, tn), lambda l, j: (l, 0, j)),
        ],
        out_specs=pl.BlockSpec((1, rows, tn), lambda l, j: (l, 0, j)),
        out_shape=jax.ShapeDtypeStruct((depth, rows, n), F32),
        compiler_params=pltpu.CompilerParams(
            dimension_semantics=("arbitrary", "arbitrary"),
            vmem_limit_bytes=VMEM_LIMIT_BYTES),
    )(c_pad, ada_w, ada_b.reshape(depth, 1, n))


def _ffn_kernel(x_ref, mod_ref, w1_ref, w3_ref, w2_ref, lng_ref, lnb_ref, o_ref, *, alpha, f_chunk):
    x = x_ref[0]
    shift = mod_ref[0, 0:1, :]
    scale = mod_ref[0, 1:2, :]
    gate = mod_ref[0, 2:3, :]
    h = (x * (1.0 + scale) + shift).astype(BF16)
    d_ff = w1_ref.shape[1]
    acc = jnp.zeros(x.shape, F32)
    for f0 in range(0, d_ff, f_chunk):
        a = jnp.dot(h, w1_ref[:, f0:f0 + f_chunk], preferred_element_type=F32)
        b = jnp.dot(h, w3_ref[:, f0:f0 + f_chunk], preferred_element_type=F32)
        g = (_silu(a) * b).astype(BF16)
        acc = acc + jnp.dot(g, w2_ref[f0:f0 + f_chunk, :], preferred_element_type=F32)
    y = alpha * x + (MACARON_HALF * gate) * acc
    o_ref[0] = _layer_norm_rows(y, lng_ref[...], lnb_ref[...])


def _ffn_call(x, mod3, w1, w3, w2, ln_g, ln_b, *, alpha, tm, f_chunk):
    bsz, seq, d = x.shape
    d_ff = w1.shape[1]
    return pl.pallas_call(
        functools.partial(_ffn_kernel, alpha=alpha, f_chunk=f_chunk),
        grid=(bsz, seq // tm),
        in_specs=[
            pl.BlockSpec((1, tm, d), lambda b, i: (b, i, 0)),
            pl.BlockSpec((1, 3, d), lambda b, i: (b, 0, 0)),
            _const_spec((d, d_ff)),
            _const_spec((d, d_ff)),
            _const_spec((d_ff, d)),
            _const_spec((1, d)),
            _const_spec((1, d)),
        ],
        out_specs=pl.BlockSpec((1, tm, d), lambda b, i: (b, i, 0)),
        out_shape=jax.ShapeDtypeStruct(x.shape, F32),
        compiler_params=pltpu.CompilerParams(
            dimension_semantics=("arbitrary", "arbitrary"),
            vmem_limit_bytes=VMEM_LIMIT_BYTES),
    )(x, mod3, w1, w3, w2, ln_g.reshape(1, d), ln_b.reshape(1, d))


def _swap_halves(v, lane_in_head, half):
    n = v.shape[-1]
    fwd = pltpu.roll(v, half, 1)
    bwd = pltpu.roll(v, n - half, 1)
    return jnp.where(lane_in_head < half, bwd, fwd)


def _mixer_kernel(
        x_ref, pos_ref, mod_ref, w_in_ref, conv_w_ref, conv_b_ref, wg_ref, bg_ref, lam_ref,
        lg_ref, gn_g_ref, gn_b_ref, avg_ref,
        s5_lr_ref, s5_li_ref, s5_dt_ref, s5_bre_ref, s5_bim_ref, s5_c_ref, s5_d_ref,
        wglu_ref, bglu_ref, w_out_ref, lng_ref, lnb_ref,
        o_ref,
        zs_ref, zp_ref, ycat_ref, a_ref, b_ref, tail_ref, hcar_ref,
        bu_ref, xb_ref, xcar_ref, zpow_ref, bbar_ref,
        dmask_ref, qdec_ref, kdec_ref, cdec_ref, rstate_ref, yssm_ref,
        *, alpha, tile, d_lru, d_ret, d_ssm):
    n_idx = pl.program_id(1)
    first_call_step = jnp.logical_and(pl.program_id(0) == 0, n_idx == 0)
    chunk_len = tile // SUBLANES
    n_state = bu_ref.shape[1] // 2
    lru_slabs = d_lru // LANES
    ssm_slabs = d_ssm // LANES
    ret_pairs = d_ret // LANES
    head_dim = d_ret // RET_HEADS
    half = head_dim // 2

    @pl.when(first_call_step)
    def _init_tables():
        lr = s5_lr_ref[...]
        li = s5_li_ref[...]
        dt = jnp.exp(s5_dt_ref[...])
        steps = (lax.broadcasted_iota(jnp.int32, (chunk_len, n_state), 0) + 1).astype(F32)
        mag = jnp.exp(steps * (lr * dt))
        ang = steps * (li * dt)
        zpow_ref[0] = mag * jnp.cos(ang)
        zpow_ref[1] = mag * jnp.sin(ang)
        zr = zpow_ref[0, 0:1, :]
        zi = zpow_ref[1, 0:1, :]
        den = lr * lr + li * li
        er = ((zr - 1.0) * lr + zi * li) / den
        ei = (zi * lr - (zr - 1.0) * li) / den
        bre = s5_bre_ref[...]
        bim = s5_bim_ref[...]
        bbar_ref[:, 0:n_state] = (er * bre - ei * bim).astype(BF16)
        bbar_ref[:, n_state:2 * n_state] = (er * bim + ei * bre).astype(BF16)
        tq = lax.broadcasted_iota(jnp.int32, (RET_CHUNK, 2 * RET_CHUNK), 0)
        tk = lax.broadcasted_iota(jnp.int32, (RET_CHUNK, 2 * RET_CHUNK), 1) % RET_CHUNK
        diff = (tq - tk).astype(F32)
        row = lax.broadcasted_iota(jnp.int32, (RET_CHUNK, d_ret), 0).astype(F32)
        lg_row = lg_ref[...]
        qdec_ref[...] = jnp.exp(lg_row * (row + 1.0))
        kdec_ref[...] = jnp.exp(lg_row * (RET_CHUNK - 1.0 - row))
        cdec_ref[...] = jnp.exp(lg_row * float(RET_CHUNK))
        for p in range(ret_pairs):
            lg_lo = lg_ref[0:1, p * LANES:p * LANES + 1]
            lg_hi = lg_ref[0:1, p * LANES + head_dim:p * LANES + head_dim + 1]
            col = lax.broadcasted_iota(jnp.int32, (RET_CHUNK, 2 * RET_CHUNK), 1)
            lg_pair = jnp.where(col < RET_CHUNK, lg_lo, lg_hi)
            dmask_ref[p] = jnp.where(diff >= 0.0, jnp.exp(lg_pair * jnp.maximum(diff, 0.0)), 0.0)

    @pl.when(n_idx == 0)
    def _reset_state():
        tail_ref[...] = jnp.zeros(tail_ref.shape, F32)
        hcar_ref[...] = jnp.zeros(hcar_ref.shape, F32)
        xcar_ref[...] = jnp.zeros(xcar_ref.shape, F32)
        rstate_ref[...] = jnp.zeros(rstate_ref.shape, F32)

    x = x_ref[0]
    shift = mod_ref[0, 0:1, :]
    scale = mod_ref[0, 1:2, :]
    gate = mod_ref[0, 2:3, :]
    h = (x * (1.0 + scale) + shift).astype(BF16)
    z = jnp.dot(h, w_in_ref[...], preferred_element_type=F32)
    o_lru, o_glru = 0, d_lru
    o_q, o_k, o_v, o_gret = 2 * d_lru, 2 * d_lru + d_ret, 2 * d_lru + 2 * d_ret, 2 * d_lru + 3 * d_ret
    o_ssm = 2 * d_lru + 4 * d_ret
    scan_cols = ([o_lru + i * LANES for i in range(lru_slabs)]
                 + [o_glru + i * LANES for i in range(lru_slabs)]
                 + [o_ssm + i * LANES for i in range(ssm_slabs)])
    for s_idx, c0 in enumerate(scan_cols):
        zs_ref[s_idx] = z[:, c0:c0 + LANES]
    n_scan_slabs = len(scan_cols)

    def _permute_in(j, carry):
        r0 = pl.multiple_of(j * SUBLANES, SUBLANES)
        for s_idx in range(n_scan_slabs):
            zp_ref[s_idx, pl.ds(r0, SUBLANES), :] = zs_ref[s_idx, pl.ds(j, SUBLANES, stride=chunk_len), :]
        return carry
    lax.fori_loop(0, chunk_len, _permute_in, 0, unroll=4)

    sub_iota = lax.broadcasted_iota(jnp.int32, (SUBLANES, d_lru), 0)

    u_p = jnp.concatenate([zp_ref[i] for i in range(lru_slabs)], axis=1)
    ext_blocks = []
    for dd in range(CONV_WIDTH - 1, 0, -1):
        cur = u_p[tile - dd * SUBLANES:tile - (dd - 1) * SUBLANES, :]
        prev = tail_ref[(CONV_WIDTH - 1 - dd) * SUBLANES:(CONV_WIDTH - dd) * SUBLANES, :]
        ext_blocks.append(pltpu.roll(jnp.where(sub_iota == SUBLANES - 1, prev, cur), 1, 0))
    tail_ref[...] = u_p[tile - (CONV_WIDTH - 1) * SUBLANES:, :]
    ext = jnp.concatenate(ext_blocks + [u_p], axis=0)
    uc = conv_b_ref[...]
    for k in range(CONV_WIDTH):
        uc = uc + ext[k * SUBLANES:k * SUBLANES + tile, :] * conv_w_ref[k:k + 1, :]
    gz = jnp.dot(uc.astype(BF16), wg_ref[...], preferred_element_type=F32) + bg_ref[...]
    r_gate = jax.nn.sigmoid(gz[:, 0:d_lru])
    i_gate = jax.nn.sigmoid(gz[:, d_lru:2 * d_lru])
    lam = lam_ref[...]
    softplus_neg_lam = jnp.maximum(-lam, 0.0) + jnp.log(1.0 + jnp.exp(-jnp.abs(lam)))
    a_coef = jnp.exp((-LRU_C * softplus_neg_lam) * r_gate)
    a_ref[...] = a_coef
    b_ref[...] = jnp.sqrt(1.0 - a_coef * a_coef) * (i_gate * uc)

    us_p = jnp.concatenate([zp_ref[2 * lru_slabs + i] for i in range(ssm_slabs)], axis=1)
    bu_ref[...] = jnp.dot(us_p.astype(BF16), bbar_ref[...], preferred_element_type=F32)

    zr_b = jnp.broadcast_to(zpow_ref[0, 0:1, :], (SUBLANES, n_state))
    zi_b = jnp.broadcast_to(zpow_ref[1, 0:1, :], (SUBLANES, n_state))

    def _scan_step(j, carry):
        h_loc, a_cum, xr, xi = carry
        r0 = pl.multiple_of(j * SUBLANES, SUBLANES)
        a_j = a_ref[pl.ds(r0, SUBLANES), :]
        b_j = b_ref[pl.ds(r0, SUBLANES), :]
        h_loc = a_j * h_loc + b_j
        a_cum = a_j * a_cum
        a_ref[pl.ds(r0, SUBLANES), :] = a_cum
        b_ref[pl.ds(r0, SUBLANES), :] = h_loc
        br = bu_ref[pl.ds(r0, SUBLANES), 0:n_state]
        bi = bu_ref[pl.ds(r0, SUBLANES), n_state:2 * n_state]
        xr_n = zr_b * xr - zi_b * xi + br
        xi_n = zr_b * xi + zi_b * xr + bi
        bu_ref[pl.ds(r0, SUBLANES), 0:n_state] = xr_n
        bu_ref[pl.ds(r0, SUBLANES), n_state:2 * n_state] = xi_n
        return h_loc, a_cum, xr_n, xi_n

    init = (jnp.zeros((SUBLANES, d_lru), F32), jnp.ones((SUBLANES, d_lru), F32),
            jnp.zeros((SUBLANES, n_state), F32), jnp.zeros((SUBLANES, n_state), F32))
    h_last, a_last, xr_last, xi_last = lax.fori_loop(0, chunk_len, _scan_step, init, unroll=2)

    h_car = jnp.broadcast_to(hcar_ref[...], (SUBLANES, d_lru))
    h_in = h_car
    for _ in range(SUBLANES - 1):
        h_in = jnp.where(sub_iota == 0, h_car, pltpu.roll(h_last + a_last * h_in, 1, 0))
    h_end = h_last + a_last * h_in
    hcar_ref[...] = h_end[SUBLANES - 1:SUBLANES, :]

    sub_iota_s = lax.broadcasted_iota(jnp.int32, (SUBLANES, n_state), 0)
    zl_r = jnp.broadcast_to(zpow_ref[0, chunk_len - 1:chunk_len, :], (SUBLANES, n_state))
    zl_i = jnp.broadcast_to(zpow_ref[1, chunk_len - 1:chunk_len, :], (SUBLANES, n_state))
    xc_r = jnp.broadcast_to(xcar_ref[0:1, :], (SUBLANES, n_state))
    xc_i = jnp.broadcast_to(xcar_ref[1:2, :], (SUBLANES, n_state))
    xin_r, xin_i = xc_r, xc_i
    for _ in range(SUBLANES - 1):
        nr = xr_last + zl_r * xin_r - zl_i * xin_i
        ni = xi_last + zl_r * xin_i + zl_i * xin_r
        xin_r = jnp.where(sub_iota_s == 0, xc_r, pltpu.roll(nr, 1, 0))
        xin_i = jnp.where(sub_iota_s == 0, xc_i, pltpu.roll(ni, 1, 0))
    xe_r = xr_last + zl_r * xin_r - zl_i * xin_i
    xe_i = xi_last + zl_r * xin_i + zl_i * xin_r
    xcar_ref[0:1, :] = xe_r[SUBLANES - 1:SUBLANES, :]
    xcar_ref[1:2, :] = xe_i[SUBLANES - 1:SUBLANES, :]

    def _fix_step(j, carry):
        r0 = pl.multiple_of(j * SUBLANES, SUBLANES)
        h_t = b_ref[pl.ds(r0, SUBLANES), :] + a_ref[pl.ds(r0, SUBLANES), :] * h_in
        g_p = jnp.concatenate(
            [zp_ref[lru_slabs + i, pl.ds(r0, SUBLANES), :] for i in range(lru_slabs)], axis=1)
        y_lru = h_t * jax.nn.gelu(g_p)
        for i in range(lru_slabs):
            ycat_ref[i, pl.ds(j, SUBLANES, stride=chunk_len), :] = y_lru[:, i * LANES:(i + 1) * LANES]
        pr = jnp.broadcast_to(zpow_ref[0, pl.ds(j, 1), :], (SUBLANES, n_state))
        pi = jnp.broadcast_to(zpow_ref[1, pl.ds(j, 1), :], (SUBLANES, n_state))
        xr = bu_ref[pl.ds(r0, SUBLANES), 0:n_state] + pr * xin_r - pi * xin_i
        xi = bu_ref[pl.ds(r0, SUBLANES), n_state:2 * n_state] + pr * xin_i + pi * xin_r
        xb_ref[pl.ds(r0, SUBLANES), 0:n_state] = xr
        xb_ref[pl.ds(r0, SUBLANES), n_state:2 * n_state] = xi
        return carry
    lax.fori_loop(0, chunk_len, _fix_step, 0, unroll=2)

    y_s = jnp.dot(xb_ref[...].astype(BF16), s5_c_ref[...], preferred_element_type=F32)
    y_s = jax.nn.gelu(y_s + s5_d_ref[...] * us_p)
    glu = jnp.dot(y_s.astype(BF16), wglu_ref[...], preferred_element_type=F32) + bglu_ref[...]
    yssm_ref[...] = y_s * jax.nn.sigmoid(glu)

    def _permute_out(j, carry):
        r0 = pl.multiple_of(j * SUBLANES, SUBLANES)
        for i in range(ssm_slabs):
            ycat_ref[lru_slabs + ret_pairs + i, pl.ds(j, SUBLANES, stride=chunk_len), :] = (
                yssm_ref[pl.ds(r0, SUBLANES), i * LANES:(i + 1) * LANES])
        return carry
    lax.fori_loop(0, chunk_len, _permute_out, 0, unroll=4)

    pos_row = pos_ref[0].astype(F32)
    freq = lax.broadcasted_iota(jnp.int32, (half, 1), 0).astype(F32)
    inv = jnp.exp(freq * (-math.log(ROPE_BASE) / half))
    ang_t = inv * pos_row
    cos_t = jnp.cos(ang_t)
    sin_t = jnp.sin(ang_t)
    reps = LANES // head_dim
    cos_tab = jnp.concatenate([cos_t, cos_t] * reps, axis=0).T
    sin_tab = jnp.concatenate([-sin_t, sin_t] * reps, axis=0).T
    lane_in_head = lax.broadcasted_iota(jnp.int32, (RET_CHUNK, LANES), 1) % head_dim
    lane = lax.broadcasted_iota(jnp.int32, (RET_CHUNK, LANES), 1)
    lo_mask = lane < head_dim
    blk_r = lax.broadcasted_iota(jnp.int32, (LANES, LANES), 0) // head_dim
    blk_c = lax.broadcasted_iota(jnp.int32, (LANES, LANES), 1) // head_dim
    same_head = blk_r == blk_c
    inv_sqrt_dh = head_dim ** -0.5
    o_parts = []
    for p in range(ret_pairs):
        c_q = o_q + p * LANES
        c_k = o_k + p * LANES
        c_v = o_v + p * LANES
        state = rstate_ref[p]
        qd = qdec_ref[:, p * LANES:(p + 1) * LANES]
        kd = kdec_ref[:, p * LANES:(p + 1) * LANES]
        cd = cdec_ref[:, p * LANES:(p + 1) * LANES]
        o_chunks = []
        for ci in range(tile // RET_CHUNK):
            rows = slice(ci * RET_CHUNK, (ci + 1) * RET_CHUNK)
            cs = cos_tab[rows]
            sn = sin_tab[rows]
            q_c = z[rows, c_q:c_q + LANES]
            k_c = z[rows, c_k:c_k + LANES]
            v_c = z[rows, c_v:c_v + LANES]
            qr = q_c * cs + _swap_halves(q_c, lane_in_head, half) * sn
            kr = (k_c * cs + _swap_halves(k_c, lane_in_head, half) * sn) * inv_sqrt_dh
            k_bd = jnp.concatenate([jnp.where(lo_mask, kr, 0.0), jnp.where(lo_mask, 0.0, kr)], axis=0)
            v_bd = jnp.concatenate([jnp.where(lo_mask, v_c, 0.0), jnp.where(lo_mask, 0.0, v_c)], axis=0)
            scores = lax.dot_general(qr.astype(BF16), k_bd.astype(BF16), (((1,), (1,)), ((), ())),
                                     preferred_element_type=F32)
            scores = scores * dmask_ref[p]
            o_c = jnp.dot(scores.astype(BF16), v_bd.astype(BF16), preferred_element_type=F32)
            o_c = o_c + jnp.dot((qr * qd).astype(BF16), state.astype(BF16), preferred_element_type=F32)
            kv = jnp.dot((kr * kd).T.astype(BF16), v_c.astype(BF16), preferred_element_type=F32)
            state = cd * state + jnp.where(same_head, kv, 0.0)
            o_chunks.append(o_c)
        rstate_ref[p] = state
        o_parts.append(jnp.concatenate(o_chunks, axis=0))
    o_all = jnp.concatenate(o_parts, axis=1)
    avg = avg_ref[...]
    o_hi = o_all.astype(BF16)
    o_lo = (o_all - o_hi.astype(F32)).astype(BF16)
    mu = jnp.dot(o_hi, avg, preferred_element_type=F32) + jnp.dot(o_lo, avg, preferred_element_type=F32)
    dev = o_all - mu
    d2 = dev * dev
    d2_hi = d2.astype(BF16)
    d2_lo = (d2 - d2_hi.astype(F32)).astype(BF16)
    var = jnp.dot(d2_hi, avg, preferred_element_type=F32) + jnp.dot(d2_lo, avg, preferred_element_type=F32)
    o_n = dev * lax.rsqrt(var + LN_EPS) * gn_g_ref[...] + gn_b_ref[...]
    y_ret = _silu(z[:, o_gret:o_gret + d_ret]) * o_n
    for i in range(ret_pairs):
        ycat_ref[lru_slabs + i] = y_ret[:, i * LANES:(i + 1) * LANES]

    n_cat = lru_slabs + ret_pairs + ssm_slabs
    ycat = jnp.concatenate([ycat_ref[i].astype(BF16) for i in range(n_cat)], axis=1)
    m = jnp.dot(ycat, w_out_ref[...], preferred_element_type=F32)
    y = alpha * x + gate * m
    o_ref[0] = _layer_norm_rows(y, lng_ref[...], lnb_ref[...])


def _mixer_call(x, pos3, mod3, prm, *, alpha, tile):
    bsz, seq, d = x.shape
    d_lru = prm["lam"].shape[1]
    d_ret = prm["gn_g"].shape[1]
    d_ssm = prm["s5_d"].shape[1]
    n_state = prm["s5_lr"].shape[1]
    n_in = prm["w_in"].shape[1]
    n_scan_slabs = (2 * d_lru + d_ssm) // LANES
    n_cat = (d_lru + d_ret + d_ssm) // LANES
    chunk_len = tile // SUBLANES
    names = ["w_in", "conv_w", "conv_b", "wg", "bg", "lam", "lg", "gn_g", "gn_b", "avg",
             "s5_lr", "s5_li", "s5_dt", "s5_bre", "s5_bim", "s5_c", "s5_d", "wglu", "bglu",
             "w_out", "ln_g", "ln_b"]
    consts = [prm[k] for k in names]
    scratch = [
        pltpu.VMEM((n_scan_slabs, tile, LANES), F32),
        pltpu.VMEM((n_scan_slabs, tile, LANES), F32),
        pltpu.VMEM((n_cat, tile, LANES), F32),
        pltpu.VMEM((tile, d_lru), F32),
        pltpu.VMEM((tile, d_lru), F32),
        pltpu.VMEM(((CONV_WIDTH - 1) * SUBLANES, d_lru), F32),
        pltpu.VMEM((1, d_lru), F32),
        pltpu.VMEM((tile, 2 * n_state), F32),
        pltpu.VMEM((tile, 2 * n_state), F32),
        pltpu.VMEM((2, n_state), F32),
        pltpu.VMEM((2, chunk_len, n_state), F32),
        pltpu.VMEM((d_ssm, 2 * n_state), BF16),
        pltpu.VMEM((d_ret // LANES, RET_CHUNK, 2 * RET_CHUNK), F32),
        pltpu.VMEM((RET_CHUNK, d_ret), F32),
        pltpu.VMEM((RET_CHUNK, d_ret), F32),
        pltpu.VMEM((1, d_ret), F32),
        pltpu.VMEM((d_ret // LANES, LANES, LANES), F32),
        pltpu.VMEM((tile, d_ssm), F32),
    ]
    return pl.pallas_call(
        functools.partial(_mixer_kernel, alpha=alpha, tile=tile, d_lru=d_lru, d_ret=d_ret, d_ssm=d_ssm),
        grid=(bsz, seq // tile),
        in_specs=[
            pl.BlockSpec((1, tile, d), lambda b, i: (b, i, 0)),
            pl.BlockSpec((1, 1, tile), lambda b, i: (b * (seq // tile) + i, 0, 0)),
            pl.BlockSpec((1, 3, d), lambda b, i: (b, 0, 0)),
        ] + [_const_spec(a.shape) for a in consts],
        out_specs=pl.BlockSpec((1, tile, d), lambda b, i: (b, i, 0)),
        out_shape=jax.ShapeDtypeStruct(x.shape, F32),
        scratch_shapes=scratch,
        compiler_params=pltpu.CompilerParams(
            dimension_semantics=("arbitrary", "arbitrary"),
            vmem_limit_bytes=VMEM_LIMIT_BYTES),
    )(x, pos3, mod3, *consts)


def _block_diag(blocks):
    n, r, c = blocks.shape
    eye = jnp.eye(n, dtype=blocks.dtype)
    return (eye[:, None, :, None] * blocks[:, :, None, :]).reshape(n * r, n * c)


def _mixer_params(l, d, mix_w_in, conv_w, conv_b, lru_wa, lru_ba, lru_wx, lru_bx, lru_lam,
                  ret_gn_g, ret_gn_b, ssm_lam_re, ssm_lam_im, ssm_log_step, ssm_b_re, ssm_b_im,
                  ssm_c_re, ssm_c_im, ssm_d, ssm_w_glu, ssm_b_glu, mix_w_out, ln_g, ln_b):
    d_lru = lru_lam.shape[1]
    d_ret = ret_gn_g.shape[1]
    d_ssm = ssm_d.shape[1]
    groups, n_per = ssm_lam_re.shape[1], ssm_lam_re.shape[2]
    n_state = groups * n_per
    head_dim = d_ret // RET_HEADS
    log_gamma = np.array([math.log1p(-2.0 ** (-5.0 - hh)) for hh in range(RET_HEADS)], np.float64)
    lg_lane = jnp.asarray(np.repeat(log_gamma, head_dim)[None, :], F32)
    avg = jnp.asarray(np.kron(np.eye(RET_HEADS), np.full((head_dim, head_dim), 1.0 / head_dim)), BF16)
    b_re_bd = _block_diag(jnp.swapaxes(ssm_b_re[l], 1, 2))
    b_im_bd = _block_diag(jnp.swapaxes(ssm_b_im[l], 1, 2))
    c_re_bd = _block_diag(jnp.swapaxes(ssm_c_re[l], 1, 2))
    c_im_bd = _block_diag(jnp.swapaxes(ssm_c_im[l], 1, 2))
    return {
        "w_in": mix_w_in[l].astype(BF16),
        "conv_w": conv_w[l],
        "conv_b": conv_b[l].reshape(1, d_lru),
        "wg": jnp.concatenate([_block_diag(lru_wa[l]), _block_diag(lru_wx[l])], axis=1).astype(BF16),
        "bg": jnp.concatenate([lru_ba[l], lru_bx[l]]).reshape(1, 2 * d_lru),
        "lam": lru_lam[l].reshape(1, d_lru),
        "lg": lg_lane,
        "gn_g": ret_gn_g[l].reshape(1, d_ret),
        "gn_b": ret_gn_b[l].reshape(1, d_ret),
        "avg": avg,
        "s5_lr": ssm_lam_re[l].reshape(1, n_state),
        "s5_li": ssm_lam_im[l].reshape(1, n_state),
        "s5_dt": jnp.repeat(ssm_log_step[l], n_per).reshape(1, n_state),
        "s5_bre": b_re_bd,
        "s5_bim": b_im_bd,
        "s5_c": jnp.concatenate([c_re_bd, -c_im_bd], axis=0).astype(BF16),
        "s5_d": ssm_d[l].reshape(1, d_ssm),
        "wglu": ssm_w_glu[l].astype(BF16),
        "bglu": ssm_b_glu[l].reshape(1, d_ssm),
        "w_out": mix_w_out[l].astype(BF16),
        "ln_g": ln_g[l, 1].reshape(1, d),
        "ln_b": ln_b[l, 1].reshape(1, d),
    }


def _tiles(seq):
    tm = 512 if seq % 512 == 0 else seq
    tile = 256 if seq % 256 == 0 else seq
    return tm, tile


def kernel(x, c, positions, ada_w, ada_b, ln_g, ln_b, ffn1_w1, ffn1_w3, ffn1_w2, mix_w_in, conv_w, conv_b, lru_wa, lru_ba, lru_wx, lru_bx, lru_lam, ret_gn_g, ret_gn_b, ssm_lam_re, ssm_lam_im, ssm_log_step, ssm_b_re, ssm_b_im, ssm_c_re, ssm_c_im, ssm_d, ssm_w_glu, ssm_b_glu, mix_w_out, ffn2_w1, ffn2_w3, ffn2_w2):
    bsz, seq, d = x.shape
    depth = ada_w.shape[0]
    alpha = (2.0 * depth) ** 0.25
    tm, tile = _tiles(seq)
    f_chunk = 256

    rows = -(-bsz // SUBLANES) * SUBLANES
    c_pad = jnp.pad(c, ((0, rows - bsz), (0, 0)))
    mod = _ada_call(c_pad, ada_w, ada_b)[:, :bsz, :].reshape(depth, bsz, N_MOD, d)
    pos3 = positions.reshape(bsz * (seq // tile), 1, tile)

    for l in range(depth):
        x = _ffn_call(x, mod[l, :, 0:3], ffn1_w1[l].astype(BF16), ffn1_w3[l].astype(BF16),
                      ffn1_w2[l].astype(BF16), ln_g[l, 0], ln_b[l, 0], alpha=alpha, tm=tm, f_chunk=f_chunk)
        prm = _mixer_params(l, d, mix_w_in, conv_w, conv_b, lru_wa, lru_ba, lru_wx, lru_bx, lru_lam,
                            ret_gn_g, ret_gn_b, ssm_lam_re, ssm_lam_im, ssm_log_step, ssm_b_re, ssm_b_im,
                            ssm_c_re, ssm_c_im, ssm_d, ssm_w_glu, ssm_b_glu, mix_w_out, ln_g, ln_b)
        x = _mixer_call(x, pos3, mod[l, :, 3:6], prm, alpha=alpha, tile=tile)
        x = _ffn_call(x, mod[l, :, 6:9], ffn2_w1[l].astype(BF16), ffn2_w3[l].astype(BF16),
                      ffn2_w2[l].astype(BF16), ln_g[l, 2], ln_b[l, 2], alpha=alpha, tm=tm, f_chunk=f_chunk)
    return x
```

```python
import functools
import math

import numpy as np
import jax
import jax.numpy as jnp
from jax import lax
from jax.experimental import pallas as pl
from jax.experimental.pallas import tpu as pltpu

F32 = jnp.float32
BF16 = jnp.bfloat16

LRU_HEADS = 6
CONV_WIDTH = 4
LRU_C = 8.0
RET_HEADS = 6
RET_CHUNK = 128
ROPE_BASE = 10000.0
SSM_GROUP = 16
SSM_STATE = 64
N_MOD = 9
MACARON_HALF = 0.5
LN_EPS = 1e-5

LANES = 128
SUBLANES = 8
VMEM_LIMIT_BYTES = 56 * 1024 * 1024
SCAN_BLOCK = SUBLANES * SUBLANES


def _silu(v):
    return v * jax.nn.sigmoid(v)


def _layer_norm_rows(v, g, b):
    mu = jnp.mean(v, axis=-1, keepdims=True)
    d = v - mu
    var = jnp.mean(d * d, axis=-1, keepdims=True)
    return d * lax.rsqrt(var + LN_EPS) * g + b


def _const_spec(shape):
    zeros = (0,) * len(shape)
    return pl.BlockSpec(shape, lambda *_: zeros, pipeline_mode=pl.Buffered(1))


def _ada_kernel(c_ref, w_ref, b_ref, o_ref):
    cond = _silu(c_ref[...]).astype(BF16)
    o_ref[0] = jnp.dot(cond, w_ref[0].astype(BF16), preferred_element_type=F32) + b_ref[0]


def _ada_call(c_pad, ada_w, ada_b):
    depth, d, n = ada_w.shape
    rows = c_pad.shape[0]
    tn = n // N_MOD
    return pl.pallas_call(
        _ada_kernel,
        grid=(depth, n // tn),
        in_specs=[
            pl.BlockSpec((rows, d), lambda l, j: (0, 0)),
            pl.BlockSpec((1, d, tn), lambda l, j: (l, 0, j)),
            pl.BlockSpec((1, 1, tn), lambda l, j: (l, 0, j)),
        ],
        out_specs=pl.BlockSpec((1, rows, tn), lambda l, j: (l, 0, j)),
        out_shape=jax.ShapeDtypeStruct((depth, rows, n), F32),
        compiler_params=pltpu.CompilerParams(
            dimension_semantics=("arbitrary", "arbitrary"),
            vmem_limit_bytes=VMEM_LIMIT_BYTES),
    )(c_pad, ada_w, ada_b.reshape(depth, 1, n))


def _ffn_kernel(x_ref, mod_ref, w1_ref, w3_ref, w2_ref, lng_ref, lnb_ref, o_ref, *, alpha, f_chunk):
    x = x_ref[0]
    shift = mod_ref[0, 0:1, :]
    scale = mod_ref[0, 1:2, :]
    gate = mod_ref[0, 2:3, :]
    h = (x * (1.0 + scale) + shift).astype(BF16)
    d_ff = w1_ref.shape[1]
    acc = jnp.zeros(x.shape, F32)
    for f0 in range(0, d_ff, f_chunk):
        a = jnp.dot(h, w1_ref[:, f0:f0 + f_chunk], preferred_element_type=F32)
        b = jnp.dot(h, w3_ref[:, f0:f0 + f_chunk], preferred_element_type=F32)
        g = (_silu(a) * b).astype(BF16)
        acc = acc + jnp.dot(g, w2_ref[f0:f0 + f_chunk, :], preferred_element_type=F32)
    y = alpha * x + (MACARON_HALF * gate) * acc
    o_ref[0] = _layer_norm_rows(y, lng_ref[...], lnb_ref[...])


def _ffn_call(x, mod3, w1, w3, w2, ln_g, ln_b, *, alpha, tm, f_chunk):
    bsz, seq, d = x.shape
    d_ff = w1.shape[1]
    return pl.pallas_call(
        functools.partial(_ffn_kernel, alpha=alpha, f_chunk=f_chunk),
        grid=(bsz, seq // tm),
        in_specs=[
            pl.BlockSpec((1, tm, d), lambda b, i: (b, i, 0)),
            pl.BlockSpec((1, 3, d), lambda b, i: (b, 0, 0)),
            _const_spec((d, d_ff)),
            _const_spec((d, d_ff)),
            _const_spec((d_ff, d)),
            _const_spec((1, d)),
            _const_spec((1, d)),
        ],
        out_specs=pl.BlockSpec((1, tm, d), lambda b, i: (b, i, 0)),
        out_shape=jax.ShapeDtypeStruct(x.shape, F32),
        compiler_params=pltpu.CompilerParams(
            dimension_semantics=("arbitrary", "arbitrary"),
            vmem_limit_bytes=VMEM_LIMIT_BYTES),
    )(x, mod3, w1, w3, w2, ln_g.reshape(1, d), ln_b.reshape(1, d))


_ZT_STEP_RE = 0
_ZT_STEP_IM = SUBLANES
_ZT_HS_RE = 2 * SUBLANES
_ZT_HS_IM = 2 * SUBLANES + 3
_ZT_CHUNK_RE = 2 * SUBLANES + 6
_ZT_CHUNK_IM = 2 * SUBLANES + 7
_ZT_ROWS = 2 * SUBLANES + 8
_HS_SHIFTS = (1, 2, 4)


def _swap_halves(v, lane_in_head, half):
    n = v.shape[-1]
    fwd = pltpu.roll(v, half, 1)
    bwd = pltpu.roll(v, n - half, 1)
    return jnp.where(lane_in_head < half, bwd, fwd)


def _row_bcast(v, row):
    return jnp.broadcast_to(v[row:row + 1, :], v.shape)


def _mixer_kernel(
        x_ref, pos_ref, mod_ref, w_in_ref, conv_w_ref, conv_b_ref, wg_ref, bg_ref, lam_ref,
        lg_ref, gn_g_ref, gn_b_ref, avg_ref,
        s5_lr_ref, s5_li_ref, s5_dt_ref, s5_bre_ref, s5_bim_ref, s5_c_ref, s5_d_ref,
        wglu_ref, bglu_ref, w_out_ref, lng_ref, lnb_ref,
        o_ref,
        zs_ref, ycat_ref, tail_ref, hcar_ref, bu_ref, xb_ref, xcar_ref, ztab_ref, bbar_ref,
        dmask_ref, qdec_ref, kdec_ref, cdec_ref, rstate_ref,
        *, alpha, tile, d_lru, d_ret, d_ssm):
    n_idx = pl.program_id(1)
    first_call_step = jnp.logical_and(pl.program_id(0) == 0, n_idx == 0)
    n_state = bu_ref.shape[1] // 2
    n_blk = tile // SCAN_BLOCK
    lru_slabs = d_lru // LANES
    ssm_slabs = d_ssm // LANES
    ret_pairs = d_ret // LANES
    head_dim = d_ret // RET_HEADS
    half = head_dim // 2

    @pl.when(first_call_step)
    def _init_tables():
        lr = s5_lr_ref[...]
        li = s5_li_ref[...]
        dt = jnp.exp(s5_dt_ref[...])
        sub = lax.broadcasted_iota(jnp.int32, (SUBLANES, n_state), 0)

        def z_power(n):
            mag = jnp.exp(n * (lr * dt))
            ang = n * (li * dt)
            return mag * jnp.cos(ang), mag * jnp.sin(ang)

        st_r, st_i = z_power((sub + 1).astype(F32))
        ch_r, ch_i = z_power((SUBLANES * (sub + 1)).astype(F32))
        for b in range(SUBLANES):
            ztab_ref[_ZT_STEP_RE + b] = _row_bcast(st_r, b)
            ztab_ref[_ZT_STEP_IM + b] = _row_bcast(st_i, b)
        for n, d in enumerate(_HS_SHIFTS):
            ztab_ref[_ZT_HS_RE + n] = jnp.where(sub >= d, _row_bcast(ch_r, d - 1), 0.0)
            ztab_ref[_ZT_HS_IM + n] = jnp.where(sub >= d, _row_bcast(ch_i, d - 1), 0.0)
        ztab_ref[_ZT_CHUNK_RE] = ch_r
        ztab_ref[_ZT_CHUNK_IM] = ch_i
        zr = st_r[0:1, :]
        zi = st_i[0:1, :]
        den = lr * lr + li * li
        er = ((zr - 1.0) * lr + zi * li) / den
        ei = (zi * lr - (zr - 1.0) * li) / den
        bre = s5_bre_ref[...]
        bim = s5_bim_ref[...]
        bbar_ref[:, 0:n_state] = (er * bre - ei * bim).astype(BF16)
        bbar_ref[:, n_state:2 * n_state] = (er * bim + ei * bre).astype(BF16)
        tq = lax.broadcasted_iota(jnp.int32, (RET_CHUNK, 2 * RET_CHUNK), 0)
        tk = lax.broadcasted_iota(jnp.int32, (RET_CHUNK, 2 * RET_CHUNK), 1) % RET_CHUNK
        diff = (tq - tk).astype(F32)
        row = lax.broadcasted_iota(jnp.int32, (RET_CHUNK, d_ret), 0).astype(F32)
        lg_row = lg_ref[...]
        qdec_ref[...] = jnp.exp(lg_row * (row + 1.0))
        kdec_ref[...] = jnp.exp(lg_row * (RET_CHUNK - 1.0 - row))
        cdec_ref[...] = jnp.exp(lg_row * float(RET_CHUNK))
        for p in range(ret_pairs):
            lg_lo = lg_ref[0:1, p * LANES:p * LANES + 1]
            lg_hi = lg_ref[0:1, p * LANES + head_dim:p * LANES + head_dim + 1]
            col = lax.broadcasted_iota(jnp.int32, (RET_CHUNK, 2 * RET_CHUNK), 1)
            lg_pair = jnp.where(col < RET_CHUNK, lg_lo, lg_hi)
            dmask_ref[p] = jnp.where(diff >= 0.0, jnp.exp(lg_pair * jnp.maximum(diff, 0.0)), 0.0)

    @pl.when(n_idx == 0)
    def _reset_state():
        tail_ref[...] = jnp.zeros(tail_ref.shape, F32)
        hcar_ref[...] = jnp.zeros(hcar_ref.shape, F32)
        xcar_ref[...] = jnp.zeros(xcar_ref.shape, F32)
        rstate_ref[...] = jnp.zeros(rstate_ref.shape, F32)

    x = x_ref[0]
    shift = mod_ref[0, 0:1, :]
    scale = mod_ref[0, 1:2, :]
    gate = mod_ref[0, 2:3, :]
    h = (x * (1.0 + scale) + shift).astype(BF16)
    z = jnp.dot(h, w_in_ref[...], preferred_element_type=F32)
    o_lru, o_glru = 0, d_lru
    o_q, o_k, o_v, o_gret = 2 * d_lru, 2 * d_lru + d_ret, 2 * d_lru + 2 * d_ret, 2 * d_lru + 3 * d_ret
    o_ssm = 2 * d_lru + 4 * d_ret
    scan_cols = ([o_lru + i * LANES for i in range(lru_slabs)]
                 + [o_ssm + i * LANES for i in range(ssm_slabs)])
    for s_idx, c0 in enumerate(scan_cols):
        zs_ref[s_idx] = z[:, c0:c0 + LANES]

    def scan_rows(k, b):
        return pl.ds(k * SCAN_BLOCK + b, SUBLANES, stride=SUBLANES)

    def blk_rows(k, b):
        return slice(k * SCAN_BLOCK + b * SUBLANES, k * SCAN_BLOCK + (b + 1) * SUBLANES)

    u_rows = [[jnp.concatenate([zs_ref[i, scan_rows(k, b), :] for i in range(lru_slabs)], axis=1)
               for b in range(SUBLANES)] for k in range(n_blk)]
    us_rows = [[jnp.concatenate([zs_ref[lru_slabs + i, scan_rows(k, b), :] for i in range(ssm_slabs)], axis=1)
                for b in range(SUBLANES)] for k in range(n_blk)]

    sub_l = lax.broadcasted_iota(jnp.int32, (SUBLANES, d_lru), 0)
    sub_s = lax.broadcasted_iota(jnp.int32, (SUBLANES, n_state), 0)

    n_tail = CONV_WIDTH - 1
    uc_rows = []
    for k in range(n_blk):
        early = {}
        for dd in range(1, n_tail + 1):
            cur = u_rows[k][SUBLANES - dd]
            if k == 0:
                prev = tail_ref[(n_tail - dd) * SUBLANES:(n_tail - dd + 1) * SUBLANES, :]
            else:
                prev = u_rows[k - 1][SUBLANES - dd]
            early[-dd] = pltpu.roll(jnp.where(sub_l == SUBLANES - 1, prev, cur), 1, 0)
        for b in range(SUBLANES):
            acc = conv_b_ref[...]
            for kk in range(CONV_WIDTH):
                src = b - n_tail + kk
                acc = acc + (u_rows[k][src] if src >= 0 else early[src]) * conv_w_ref[kk:kk + 1, :]
            uc_rows.append(acc)
    for dd in range(1, n_tail + 1):
        tail_ref[(n_tail - dd) * SUBLANES:(n_tail - dd + 1) * SUBLANES, :] = u_rows[n_blk - 1][SUBLANES - dd]
    uc = jnp.concatenate(uc_rows, axis=0)
    gz = jnp.dot(uc.astype(BF16), wg_ref[...], preferred_element_type=F32) + bg_ref[...]
    r_gate = jax.nn.sigmoid(gz[:, 0:d_lru])
    i_gate = jax.nn.sigmoid(gz[:, d_lru:2 * d_lru])
    lam = lam_ref[...]
    softplus_neg_lam = jnp.maximum(-lam, 0.0) + jnp.log(1.0 + jnp.exp(-jnp.abs(lam)))
    a_coef = jnp.exp((-LRU_C * softplus_neg_lam) * r_gate)
    b_term = jnp.sqrt(1.0 - a_coef * a_coef) * (i_gate * uc)

    h_car = jnp.broadcast_to(hcar_ref[...], (SUBLANES, d_lru))
    for k in range(n_blk):
        h_loc = [b_term[blk_rows(k, 0)]]
        a_cum = [a_coef[blk_rows(k, 0)]]
        for b in range(1, SUBLANES):
            a_b = a_coef[blk_rows(k, b)]
            h_loc.append(a_b * h_loc[-1] + b_term[blk_rows(k, b)])
            a_cum.append(a_b * a_cum[-1])
        pa, pb = a_cum[-1], h_loc[-1]
        for d in _HS_SHIFTS:
            live = sub_l >= d
            pb = jnp.where(live, pa * pltpu.roll(pb, d, 0) + pb, pb)
            pa = jnp.where(live, pa * pltpu.roll(pa, d, 0), pa)
        ends = pb + pa * h_car
        h_in = jnp.where(sub_l == 0, h_car, pltpu.roll(ends, 1, 0))
        h_car = _row_bcast(ends, SUBLANES - 1)
        for b in range(SUBLANES):
            h_t = h_loc[b] + a_cum[b] * h_in
            for i in range(lru_slabs):
                ycat_ref[i, scan_rows(k, b), :] = h_t[:, i * LANES:(i + 1) * LANES]
    hcar_ref[...] = h_car[0:1, :]

    us_p = jnp.concatenate([us_rows[k][b] for k in range(n_blk) for b in range(SUBLANES)], axis=0)
    bu_ref[...] = jnp.dot(us_p.astype(BF16), bbar_ref[...], preferred_element_type=F32)
    re_cols = slice(0, n_state)
    im_cols = slice(n_state, 2 * n_state)
    zr = ztab_ref[_ZT_STEP_RE]
    zi = ztab_ref[_ZT_STEP_IM]
    xc_r = jnp.broadcast_to(xcar_ref[0:1, :], (SUBLANES, n_state))
    xc_i = jnp.broadcast_to(xcar_ref[1:2, :], (SUBLANES, n_state))
    for k in range(n_blk):
        xr = bu_ref[blk_rows(k, 0), re_cols]
        xi = bu_ref[blk_rows(k, 0), im_cols]
        for b in range(1, SUBLANES):
            xr, xi = (zr * xr - zi * xi + bu_ref[blk_rows(k, b), re_cols],
                      zr * xi + zi * xr + bu_ref[blk_rows(k, b), im_cols])
            bu_ref[blk_rows(k, b), re_cols] = xr
            bu_ref[blk_rows(k, b), im_cols] = xi
        pr, pi = xr, xi
        for n, d in enumerate(_HS_SHIFTS):
            hr, hi = ztab_ref[_ZT_HS_RE + n], ztab_ref[_ZT_HS_IM + n]
            sr, si = pltpu.roll(pr, d, 0), pltpu.roll(pi, d, 0)
            pr, pi = pr + hr * sr - hi * si, pi + hr * si + hi * sr
        cr, ci = ztab_ref[_ZT_CHUNK_RE], ztab_ref[_ZT_CHUNK_IM]
        end_r = pr + cr * xc_r - ci * xc_i
        end_i = pi + cr * xc_i + ci * xc_r
        xin_r = jnp.where(sub_s == 0, xc_r, pltpu.roll(end_r, 1, 0))
        xin_i = jnp.where(sub_s == 0, xc_i, pltpu.roll(end_i, 1, 0))
        xc_r = _row_bcast(end_r, SUBLANES - 1)
        xc_i = _row_bcast(end_i, SUBLANES - 1)
        for b0 in range(0, SUBLANES, 2):
            halves_r, halves_i = [], []
            for b in (b0, b0 + 1):
                sr, si = ztab_ref[_ZT_STEP_RE + b], ztab_ref[_ZT_STEP_IM + b]
                halves_r.append(bu_ref[blk_rows(k, b), re_cols] + sr * xin_r - si * xin_i)
                halves_i.append(bu_ref[blk_rows(k, b), im_cols] + sr * xin_i + si * xin_r)
            rows2 = slice(k * SCAN_BLOCK + b0 * SUBLANES, k * SCAN_BLOCK + (b0 + 2) * SUBLANES)
            xb_ref[rows2, re_cols] = jnp.concatenate(halves_r, axis=0).astype(BF16)
            xb_ref[rows2, im_cols] = jnp.concatenate(halves_i, axis=0).astype(BF16)
    xcar_ref[0:1, :] = xc_r[0:1, :]
    xcar_ref[1:2, :] = xc_i[0:1, :]

    y_s = jnp.dot(xb_ref[...], s5_c_ref[...], preferred_element_type=F32)
    y_s = jax.nn.gelu(y_s + s5_d_ref[...] * us_p)
    glu = jnp.dot(y_s.astype(BF16), wglu_ref[...], preferred_element_type=F32) + bglu_ref[...]
    y_ssm = y_s * jax.nn.sigmoid(glu)
    for k in range(n_blk):
        for b in range(SUBLANES):
            for i in range(ssm_slabs):
                ycat_ref[lru_slabs + i, scan_rows(k, b), :] = y_ssm[blk_rows(k, b), i * LANES:(i + 1) * LANES]

    pos_row = pos_ref[0].astype(F32)
    freq = lax.broadcasted_iota(jnp.int32, (half, 1), 0).astype(F32)
    inv = jnp.exp(freq * (-math.log(ROPE_BASE) / half))
    ang_t = inv * pos_row
    cos_t = jnp.cos(ang_t)
    sin_t = jnp.sin(ang_t)
    reps = LANES // head_dim
    cos_tab = jnp.concatenate([cos_t, cos_t] * reps, axis=0).T
    sin_tab = jnp.concatenate([-sin_t, sin_t] * reps, axis=0).T
    lane_in_head = lax.broadcasted_iota(jnp.int32, (RET_CHUNK, LANES), 1) % head_dim
    lane = lax.broadcasted_iota(jnp.int32, (RET_CHUNK, LANES), 1)
    lo_mask = lane < head_dim
    blk_r = lax.broadcasted_iota(jnp.int32, (LANES, LANES), 0) // head_dim
    blk_c = lax.broadcasted_iota(jnp.int32, (LANES, LANES), 1) // head_dim
    same_head = blk_r == blk_c
    inv_sqrt_dh = head_dim ** -0.5
    o_parts = []
    for p in range(ret_pairs):
        c_q = o_q + p * LANES
        c_k = o_k + p * LANES
        c_v = o_v + p * LANES
        state = rstate_ref[p]
        qd = qdec_ref[:, p * LANES:(p + 1) * LANES]
        kd = kdec_ref[:, p * LANES:(p + 1) * LANES]
        cd = cdec_ref[:, p * LANES:(p + 1) * LANES]
        o_chunks = []
        for ci_ in range(tile // RET_CHUNK):
            rows = slice(ci_ * RET_CHUNK, (ci_ + 1) * RET_CHUNK)
            cs = cos_tab[rows]
            sn = sin_tab[rows]
            q_c = z[rows, c_q:c_q + LANES]
            k_c = z[rows, c_k:c_k + LANES]
            v_c = z[rows, c_v:c_v + LANES]
            qr = q_c * cs + _swap_halves(q_c, lane_in_head, half) * sn
            kr = (k_c * cs + _swap_halves(k_c, lane_in_head, half) * sn) * inv_sqrt_dh
            k_bd = jnp.concatenate([jnp.where(lo_mask, kr, 0.0), jnp.where(lo_mask, 0.0, kr)], axis=0)
            v_bd = jnp.concatenate([jnp.where(lo_mask, v_c, 0.0), jnp.where(lo_mask, 0.0, v_c)], axis=0)
            scores = lax.dot_general(qr.astype(BF16), k_bd.astype(BF16), (((1,), (1,)), ((), ())),
                                     preferred_element_type=F32)
            scores = scores * dmask_ref[p]
            o_c = jnp.dot(scores.astype(BF16), v_bd.astype(BF16), preferred_element_type=F32)
            o_c = o_c + jnp.dot((qr * qd).astype(BF16), state.astype(BF16), preferred_element_type=F32)
            kv = jnp.dot((kr * kd).T.astype(BF16), v_c.astype(BF16), preferred_element_type=F32)
            state = cd * state + jnp.where(same_head, kv, 0.0)
            o_chunks.append(o_c)
        rstate_ref[p] = state
        o_parts.append(jnp.concatenate(o_chunks, axis=0))
    o_all = jnp.concatenate(o_parts, axis=1)
    avg = avg_ref[...]
    o_hi = o_all.astype(BF16)
    o_lo = (o_all - o_hi.astype(F32)).astype(BF16)
    mu = jnp.dot(o_hi, avg, preferred_element_type=F32) + jnp.dot(o_lo, avg, preferred_element_type=F32)
    dev = o_all - mu
    d2 = dev * dev
    d2_hi = d2.astype(BF16)
    d2_lo = (d2 - d2_hi.astype(F32)).astype(BF16)
    var = jnp.dot(d2_hi, avg, preferred_element_type=F32) + jnp.dot(d2_lo, avg, preferred_element_type=F32)
    o_n = dev * lax.rsqrt(var + LN_EPS) * gn_g_ref[...] + gn_b_ref[...]
    y_ret = _silu(z[:, o_gret:o_gret + d_ret]) * o_n

    y_lru = [ycat_ref[i] * jax.nn.gelu(z[:, o_glru + i * LANES:o_glru + (i + 1) * LANES])
             for i in range(lru_slabs)]
    y_ssm_t = [ycat_ref[lru_slabs + i] for i in range(ssm_slabs)]
    ycat = jnp.concatenate([v.astype(BF16) for v in y_lru] + [y_ret.astype(BF16)]
                           + [v.astype(BF16) for v in y_ssm_t], axis=1)
    m = jnp.dot(ycat, w_out_ref[...], preferred_element_type=F32)
    y = alpha * x + gate * m
    o_ref[0] = _layer_norm_rows(y, lng_ref[...], lnb_ref[...])


def _mixer_call(x, pos3, mod3, prm, *, alpha, tile):
    bsz, seq, d = x.shape
    d_lru = prm["lam"].shape[1]
    d_ret = prm["gn_g"].shape[1]
    d_ssm = prm["s5_d"].shape[1]
    n_state = prm["s5_lr"].shape[1]
    n_scan_slabs = (d_lru + d_ssm) // LANES
    names = ["w_in", "conv_w", "conv_b", "wg", "bg", "lam", "lg", "gn_g", "gn_b", "avg",
             "s5_lr", "s5_li", "s5_dt", "s5_bre", "s5_bim", "s5_c", "s5_d", "wglu", "bglu",
             "w_out", "ln_g", "ln_b"]
    consts = [prm[k] for k in names]
    scratch = [
        pltpu.VMEM((n_scan_slabs, tile, LANES), F32),
        pltpu.VMEM((n_scan_slabs, tile, LANES), F32),
        pltpu.VMEM(((CONV_WIDTH - 1) * SUBLANES, d_lru), F32),
        pltpu.VMEM((1, d_lru), F32),
        pltpu.VMEM((tile, 2 * n_state), F32),
        pltpu.VMEM((tile, 2 * n_state), BF16),
        pltpu.VMEM((2, n_state), F32),
        pltpu.VMEM((_ZT_ROWS, SUBLANES, n_state), F32),
        pltpu.VMEM((d_ssm, 2 * n_state), BF16),
        pltpu.VMEM((d_ret // LANES, RET_CHUNK, 2 * RET_CHUNK), F32),
        pltpu.VMEM((RET_CHUNK, d_ret), F32),
        pltpu.VMEM((RET_CHUNK, d_ret), F32),
        pltpu.VMEM((1, d_ret), F32),
        pltpu.VMEM((d_ret // LANES, LANES, LANES), F32),
    ]
    return pl.pallas_call(
        functools.partial(_mixer_kernel, alpha=alpha, tile=tile, d_lru=d_lru, d_ret=d_ret, d_ssm=d_ssm),
        grid=(bsz, seq // tile),
        in_specs=[
            pl.BlockSpec((1, tile, d), lambda b, i: (b, i, 0)),
            pl.BlockSpec((1, 1, tile), lambda b, i: (b * (seq // tile) + i, 0, 0)),
            pl.BlockSpec((1, 3, d), lambda b, i: (b, 0, 0)),
        ] + [_const_spec(a.shape) for a in consts],
        out_specs=pl.BlockSpec((1, tile, d), lambda b, i: (b, i, 0)),
        out_shape=jax.ShapeDtypeStruct(x.shape, F32),
        scratch_shapes=scratch,
        compiler_params=pltpu.CompilerParams(
            dimension_semantics=("arbitrary", "arbitrary"),
            vmem_limit_bytes=VMEM_LIMIT_BYTES),
    )(x, pos3, mod3, *consts)


def _block_diag(blocks):
    n, r, c = blocks.shape
    eye = jnp.eye(n, dtype=blocks.dtype)
    return (eye[:, None, :, None] * blocks[:, :, None, :]).reshape(n * r, n * c)


def _mixer_params(l, d, mix_w_in, conv_w, conv_b, lru_wa, lru_ba, lru_wx, lru_bx, lru_lam,
                  ret_gn_g, ret_gn_b, ssm_lam_re, ssm_lam_im, ssm_log_step, ssm_b_re, ssm_b_im,
                  ssm_c_re, ssm_c_im, ssm_d, ssm_w_glu, ssm_b_glu, mix_w_out, ln_g, ln_b):
    d_lru = lru_lam.shape[1]
    d_ret = ret_gn_g.shape[1]
    d_ssm = ssm_d.shape[1]
    groups, n_per = ssm_lam_re.shape[1], ssm_lam_re.shape[2]
    n_state = groups * n_per
    head_dim = d_ret // RET_HEADS
    log_gamma = np.array([math.log1p(-2.0 ** (-5.0 - hh)) for hh in range(RET_HEADS)], np.float64)
    lg_lane = jnp.asarray(np.repeat(log_gamma, head_dim)[None, :], F32)
    avg = jnp.asarray(np.kron(np.eye(RET_HEADS), np.full((head_dim, head_dim), 1.0 / head_dim)), BF16)
    b_re_bd = _block_diag(jnp.swapaxes(ssm_b_re[l], 1, 2))
    b_im_bd = _block_diag(jnp.swapaxes(ssm_b_im[l], 1, 2))
    c_re_bd = _block_diag(jnp.swapaxes(ssm_c_re[l], 1, 2))
    c_im_bd = _block_diag(jnp.swapaxes(ssm_c_im[l], 1, 2))
    return {
        "w_in": mix_w_in[l].astype(BF16),
        "conv_w": conv_w[l],
        "conv_b": conv_b[l].reshape(1, d_lru),
        "wg": jnp.concatenate([_block_diag(lru_wa[l]), _block_diag(lru_wx[l])], axis=1).astype(BF16),
        "bg": jnp.concatenate([lru_ba[l], lru_bx[l]]).reshape(1, 2 * d_lru),
        "lam": lru_lam[l].reshape(1, d_lru),
        "lg": lg_lane,
        "gn_g": ret_gn_g[l].reshape(1, d_ret),
        "gn_b": ret_gn_b[l].reshape(1, d_ret),
        "avg": avg,
        "s5_lr": ssm_lam_re[l].reshape(1, n_state),
        "s5_li": ssm_lam_im[l].reshape(1, n_state),
        "s5_dt": jnp.repeat(ssm_log_step[l], n_per).reshape(1, n_state),
        "s5_bre": b_re_bd,
        "s5_bim": b_im_bd,
        "s5_c": jnp.concatenate([c_re_bd, -c_im_bd], axis=0).astype(BF16),
        "s5_d": ssm_d[l].reshape(1, d_ssm),
        "wglu": ssm_w_glu[l].astype(BF16),
        "bglu": ssm_b_glu[l].reshape(1, d_ssm),
        "w_out": mix_w_out[l].astype(BF16),
        "ln_g": ln_g[l, 1].reshape(1, d),
        "ln_b": ln_b[l, 1].reshape(1, d),
    }


def _tiles(seq):
    tm = 512 if seq % 512 == 0 else seq
    tile = 512 if seq % 512 == 0 else seq
    return tm, tile


def kernel(x, c, positions, ada_w, ada_b, ln_g, ln_b, ffn1_w1, ffn1_w3, ffn1_w2, mix_w_in, conv_w, conv_b, lru_wa, lru_ba, lru_wx, lru_bx, lru_lam, ret_gn_g, ret_gn_b, ssm_lam_re, ssm_lam_im, ssm_log_step, ssm_b_re, ssm_b_im, ssm_c_re, ssm_c_im, ssm_d, ssm_w_glu, ssm_b_glu, mix_w_out, ffn2_w1, ffn2_w3, ffn2_w2):
    bsz, seq, d = x.shape
    depth = ada_w.shape[0]
    alpha = (2.0 * depth) ** 0.25
    tm, tile = _tiles(seq)
    f_chunk = 256

    rows = -(-bsz // SUBLANES) * SUBLANES
    c_pad = jnp.pad(c, ((0, rows - bsz), (0, 0)))
    mod = _ada_call(c_pad, ada_w, ada_b)[:, :bsz, :].reshape(depth, bsz, N_MOD, d)
    pos3 = positions.reshape(bsz * (seq // tile), 1, tile)

    for l in range(depth):
        x = _ffn_call(x, mod[l, :, 0:3], ffn1_w1[l].astype(BF16), ffn1_w3[l].astype(BF16),
                      ffn1_w2[l].astype(BF16), ln_g[l, 0], ln_b[l, 0], alpha=alpha, tm=tm, f_chunk=f_chunk)
        prm = _mixer_params(l, d, mix_w_in, conv_w, conv_b, lru_wa, lru_ba, lru_wx, lru_bx, lru_lam,
                            ret_gn_g, ret_gn_b, ssm_lam_re, ssm_lam_im, ssm_log_step, ssm_b_re, ssm_b_im,
                            ssm_c_re, ssm_c_im, ssm_d, ssm_w_glu, ssm_b_glu, mix_w_out, ln_g, ln_b)
        x = _mixer_call(x, pos3, mod[l, :, 3:6], prm, alpha=alpha, tile=tile)
        x = _ffn_call(x, mod[l, :, 6:9], ffn2_w1[l].astype(BF16), ffn2_w3[l].astype(BF16),
                      ffn2_w2[l].astype(BF16), ln_g[l, 2], ln_b[l, 2], alpha=alpha, tm=tm, f_chunk=f_chunk)
    return x
```

```python
import functools
import math

import numpy as np
import jax
import jax.numpy as jnp
from jax import lax
from jax.experimental import pallas as pl
from jax.experimental.pallas import tpu as pltpu

F32 = jnp.float32
BF16 = jnp.bfloat16

LRU_HEADS = 6
CONV_WIDTH = 4
LRU_C = 8.0
RET_HEADS = 6
RET_CHUNK = 128
ROPE_BASE = 10000.0
SSM_GROUP = 16
SSM_STATE = 64
N_MOD = 9
MACARON_HALF = 0.5
LN_EPS = 1e-5

LANES = 128
SUBLANES = 8
VMEM_LIMIT_BYTES = 56 * 1024 * 1024
SCAN_BLOCK = SUBLANES * SUBLANES
MXU_DIM = 256
PROJ_CHUNK = MXU_DIM


def _silu(v):
    return v * jax.nn.sigmoid(v)


def _layer_norm_rows(v, g, b):
    mu = jnp.mean(v, axis=-1, keepdims=True)
    d = v - mu
    var = jnp.mean(d * d, axis=-1, keepdims=True)
    return d * lax.rsqrt(var + LN_EPS) * g + b


def _layer_spec(shape, layer):
    tail = (0,) * (len(shape) - 1)
    return pl.BlockSpec((None,) + tuple(shape[1:]), lambda *_: (layer,) + tail,
                        pipeline_mode=pl.Buffered(1))


def _ada_kernel(c_ref, w_ref, b_ref, o_ref):
    cond = _silu(c_ref[...]).astype(BF16)
    o_ref[0] = jnp.dot(cond, w_ref[0].astype(BF16), preferred_element_type=F32) + b_ref[0]


def _ada_call(c_pad, ada_w, ada_b):
    depth, d, n = ada_w.shape
    rows = c_pad.shape[0]
    tn = n // N_MOD
    return pl.pallas_call(
        _ada_kernel,
        grid=(depth, n // tn),
        in_specs=[
            pl.BlockSpec((rows, d), lambda l, j: (0, 0)),
            pl.BlockSpec((1, d, tn), lambda l, j: (l, 0, j)),
            pl.BlockSpec((1, 1, tn), lambda l, j: (l, 0, j)),
        ],
        out_specs=pl.BlockSpec((1, rows, tn), lambda l, j: (l, 0, j)),
        out_shape=jax.ShapeDtypeStruct((depth, rows, n), F32),
        compiler_params=pltpu.CompilerParams(
            dimension_semantics=("arbitrary", "arbitrary"),
            vmem_limit_bytes=VMEM_LIMIT_BYTES),
    )(c_pad, ada_w, ada_b.reshape(depth, 1, n))


def _ffn_kernel(x_ref, mod_ref, w1_ref, w3_ref, w2_ref, lng_ref, lnb_ref, o_ref, *, alpha, f_chunk):
    x = x_ref[0]
    shift = mod_ref[0:1, :]
    scale = mod_ref[1:2, :]
    gate = mod_ref[2:3, :]
    h = (x * (1.0 + scale) + shift).astype(BF16)
    d_ff = w1_ref.shape[1]
    acc = jnp.zeros(x.shape, F32)
    for f0 in range(0, d_ff, f_chunk):
        a = jnp.dot(h, w1_ref[:, f0:f0 + f_chunk], preferred_element_type=F32)
        b = jnp.dot(h, w3_ref[:, f0:f0 + f_chunk], preferred_element_type=F32)
        g = (_silu(a) * b).astype(BF16)
        acc = acc + jnp.dot(g, w2_ref[f0:f0 + f_chunk, :], preferred_element_type=F32)
    y = alpha * x + (MACARON_HALF * gate) * acc
    o_ref[0] = _layer_norm_rows(y, lng_ref[...], lnb_ref[...])


def _ffn_call(x, mod, w1, w3, w2, ln_g, ln_b, *, layer, sub, alpha, tm, f_chunk):
    bsz, seq, d = x.shape
    mod_rows = mod.shape[0] // (w1.shape[0] * 3)
    return pl.pallas_call(
        functools.partial(_ffn_kernel, alpha=alpha, f_chunk=f_chunk),
        grid=(bsz, seq // tm),
        in_specs=[
            pl.BlockSpec((1, tm, d), lambda b, i: (b, i, 0)),
            pl.BlockSpec((None, 3, d), lambda b, i: ((layer * mod_rows + b) * 3 + sub, 0, 0)),
            _layer_spec(w1.shape, layer),
            _layer_spec(w3.shape, layer),
            _layer_spec(w2.shape, layer),
            _layer_spec(ln_g.shape, layer * 3 + sub),
            _layer_spec(ln_b.shape, layer * 3 + sub),
        ],
        out_specs=pl.BlockSpec((1, tm, d), lambda b, i: (b, i, 0)),
        out_shape=jax.ShapeDtypeStruct(x.shape, F32),
        compiler_params=pltpu.CompilerParams(
            dimension_semantics=("arbitrary", "arbitrary"),
            vmem_limit_bytes=VMEM_LIMIT_BYTES),
    )(x, mod, w1, w3, w2, ln_g, ln_b)


_ZT_STEP_RE = 0
_ZT_STEP_IM = SUBLANES
_ZT_HS_RE = 2 * SUBLANES
_ZT_HS_IM = 2 * SUBLANES + 3
_ZT_ROWS = 2 * SUBLANES + 6
_HS_SHIFTS = (1, 2, 4)


def _swap_halves(v, lane_in_head, half):
    n = v.shape[-1]
    fwd = pltpu.roll(v, half, 1)
    bwd = pltpu.roll(v, n - half, 1)
    return jnp.where(lane_in_head < half, bwd, fwd)


def _row_bcast(v, row):
    return jnp.broadcast_to(v[row:row + 1, :], v.shape)


_N_MIXER_IO = 28


def _mixer_kernel(*refs, **static):
    io, (z_a, zs_a, z_b, zs_b), rest = refs[:_N_MIXER_IO], refs[_N_MIXER_IO:_N_MIXER_IO + 4], refs[_N_MIXER_IO + 4:]
    parity = pl.program_id(0) % 2

    @pl.when(parity == 0)
    def _even_step():
        _mixer_step(*io, z_a, zs_a, z_b, zs_b, *rest, **static)

    @pl.when(parity == 1)
    def _odd_step():
        _mixer_step(*io, z_b, zs_b, z_a, zs_a, *rest, **static)


def _mixer_step(
        xp_ref, modp_ref,
        x_ref, pos_ref, mod_ref, w_in_ref, conv_w_ref, conv_b_ref, wg_ref, bg_ref, lam_ref,
        lg_ref, gn_g_ref, gn_b_ref, avg_ref,
        s5_lr_ref, s5_li_ref, s5_dt_ref, s5_bre_ref, s5_bim_ref, s5_c_ref, s5_d_ref,
        wglu_ref, bglu_ref, w_out_ref, lng_ref, lnb_ref,
        o_ref,
        zp_ref, zsp_ref, zc_ref, zsc_ref,
        ycat_ref, tail_ref, hcar_ref, bu_ref, xb_ref, xcar_ref, ztab_ref, bbar_ref,
        dmask_ref, qdec_ref, kdec_ref, cdec_ref, rstate_ref,
        *, alpha, tile, tiles_per_seq, d_lru, d_ret, d_ssm):
    g_idx = pl.program_id(0)
    first_call_step = g_idx == 0
    n_state = bu_ref.shape[1] // 2
    n_blk = tile // SCAN_BLOCK
    lru_slabs = d_lru // LANES
    ssm_slabs = d_ssm // LANES
    ret_pairs = d_ret // LANES
    head_dim = d_ret // RET_HEADS
    half = head_dim // 2

    @pl.when(first_call_step)
    def _init_tables():
        lr = s5_lr_ref[...]
        li = s5_li_ref[...]
        dt = jnp.exp(s5_dt_ref[...])
        sub = lax.broadcasted_iota(jnp.int32, (SUBLANES, n_state), 0)

        def z_power(n):
            mag = jnp.exp(n * (lr * dt))
            ang = n * (li * dt)
            return mag * jnp.cos(ang), mag * jnp.sin(ang)

        st_r, st_i = z_power((sub + 1).astype(F32))
        ch_r, ch_i = z_power((SUBLANES * (sub + 1)).astype(F32))
        for b in range(SUBLANES):
            ztab_ref[_ZT_STEP_RE + b] = _row_bcast(st_r, b)
            ztab_ref[_ZT_STEP_IM + b] = _row_bcast(st_i, b)
        for n, d in enumerate(_HS_SHIFTS):
            ztab_ref[_ZT_HS_RE + n] = jnp.where(sub >= d, _row_bcast(ch_r, d - 1), 0.0)
            ztab_ref[_ZT_HS_IM + n] = jnp.where(sub >= d, _row_bcast(ch_i, d - 1), 0.0)
        zr = st_r[0:1, :]
        zi = st_i[0:1, :]
        den = lr * lr + li * li
        er = ((zr - 1.0) * lr + zi * li) / den
        ei = (zi * lr - (zr - 1.0) * li) / den
        bre = s5_bre_ref[...]
        bim = s5_bim_ref[...]
        bbar_ref[:, 0:n_state] = (er * bre - ei * bim).astype(BF16)
        bbar_ref[:, n_state:2 * n_state] = (er * bim + ei * bre).astype(BF16)
        tq = lax.broadcasted_iota(jnp.int32, (RET_CHUNK, 2 * RET_CHUNK), 0)
        tk = lax.broadcasted_iota(jnp.int32, (RET_CHUNK, 2 * RET_CHUNK), 1) % RET_CHUNK
        diff = (tq - tk).astype(F32)
        row = lax.broadcasted_iota(jnp.int32, (RET_CHUNK, d_ret), 0).astype(F32)
        lg_row = lg_ref[...]
        qdec_ref[...] = jnp.exp(lg_row * (row + 1.0))
        kdec_ref[...] = jnp.exp(lg_row * (RET_CHUNK - 1.0 - row))
        cdec_ref[...] = jnp.exp(lg_row * float(RET_CHUNK))
        for p in range(ret_pairs):
            lg_lo = lg_ref[0:1, p * LANES:p * LANES + 1]
            lg_hi = lg_ref[0:1, p * LANES + head_dim:p * LANES + head_dim + 1]
            col = lax.broadcasted_iota(jnp.int32, (RET_CHUNK, 2 * RET_CHUNK), 1)
            lg_pair = jnp.where(col < RET_CHUNK, lg_lo, lg_hi)
            dmask_ref[p] = jnp.where(diff >= 0.0, jnp.exp(lg_pair * jnp.maximum(diff, 0.0)), 0.0)

    @pl.when(jnp.logical_or(g_idx == 0, (g_idx - 1) % tiles_per_seq == 0))
    def _reset_state():
        tail_ref[...] = jnp.zeros(tail_ref.shape, F32)
        hcar_ref[...] = jnp.zeros(hcar_ref.shape, F32)
        xcar_ref[...] = jnp.zeros(xcar_ref.shape, F32)
        rstate_ref[...] = jnp.zeros(rstate_ref.shape, F32)

    n_scan_slabs = lru_slabs + ssm_slabs

    @pl.when(first_call_step)
    def _zero_placeholder_tile():
        zc_ref[...] = jnp.zeros(zc_ref.shape, F32)
        zsc_ref[...] = jnp.zeros(zsc_ref.shape, F32)

    o_lru, o_glru = 0, d_lru
    o_q, o_k, o_v, o_gret = 2 * d_lru, 2 * d_lru + d_ret, 2 * d_lru + 2 * d_ret, 2 * d_lru + 3 * d_ret
    o_ssm = 2 * d_lru + 4 * d_ret

    xp = xp_ref[0]
    hp = (xp * (1.0 + modp_ref[1:2, :]) + modp_ref[0:1, :]).astype(BF16)
    scan_cols = ([o_lru + i * LANES for i in range(lru_slabs)]
                 + [o_ssm + i * LANES for i in range(ssm_slabs)])
    proj_chunks = list(range(0, w_in_ref.shape[1], PROJ_CHUNK))

    def project(n_chunks):
        for _ in range(min(n_chunks, len(proj_chunks))):
            c0 = proj_chunks.pop(0)
            zchunk = jnp.dot(hp, w_in_ref[:, c0:c0 + PROJ_CHUNK], preferred_element_type=F32)
            zp_ref[:, c0:c0 + PROJ_CHUNK] = zchunk
            for s_idx, sc in enumerate(scan_cols):
                if c0 <= sc < c0 + PROJ_CHUNK:
                    zsp_ref[s_idx] = zchunk[:, sc - c0:sc - c0 + LANES]

    x = x_ref[0]
    gate = mod_ref[2:3, :]

    def z_cols(c0, width, r0=0, n_rows=tile):
        return zc_ref[r0:r0 + n_rows, c0:c0 + width]

    def scan_rows(k, b):
        return pl.ds(k * SCAN_BLOCK + b, SUBLANES, stride=SUBLANES)

    def blk_rows(k, b):
        return slice(k * SCAN_BLOCK + b * SUBLANES, k * SCAN_BLOCK + (b + 1) * SUBLANES)

    u_rows = [[jnp.concatenate([zsc_ref[i, scan_rows(k, b), :] for i in range(lru_slabs)], axis=1)
               for b in range(SUBLANES)] for k in range(n_blk)]
    us_rows = [[jnp.concatenate([zsc_ref[lru_slabs + i, scan_rows(k, b), :] for i in range(ssm_slabs)], axis=1)
                for b in range(SUBLANES)] for k in range(n_blk)]

    sub_l = lax.broadcasted_iota(jnp.int32, (SUBLANES, d_lru), 0)
    sub_s = lax.broadcasted_iota(jnp.int32, (SUBLANES, n_state), 0)

    pos_row = pos_ref[0].astype(F32)
    freq = lax.broadcasted_iota(jnp.int32, (half, 1), 0).astype(F32)
    inv = jnp.exp(freq * (-math.log(ROPE_BASE) / half))
    ang_t = inv * pos_row
    cos_t = jnp.cos(ang_t)
    sin_t = jnp.sin(ang_t)
    reps = LANES // head_dim
    cos_tab = jnp.concatenate([cos_t, cos_t] * reps, axis=0).T
    sin_tab = jnp.concatenate([-sin_t, sin_t] * reps, axis=0).T
    lane_in_head = lax.broadcasted_iota(jnp.int32, (RET_CHUNK, LANES), 1) % head_dim
    lane = lax.broadcasted_iota(jnp.int32, (RET_CHUNK, LANES), 1)
    lo_mask = lane < head_dim
    blk_r = lax.broadcasted_iota(jnp.int32, (LANES, LANES), 0) // head_dim
    blk_c = lax.broadcasted_iota(jnp.int32, (LANES, LANES), 1) // head_dim
    same_head = blk_r == blk_c
    inv_sqrt_dh = head_dim ** -0.5
    ret_state = [rstate_ref[p] for p in range(ret_pairs)]
    ret_out = [[] for _ in range(ret_pairs)]
    ret_units = [(ci_, p) for ci_ in range(tile // RET_CHUNK) for p in range(ret_pairs)]

    def retention_units(n_units):
        for _ in range(min(n_units, len(ret_units))):
            ci_, p = ret_units.pop(0)
            rows = slice(ci_ * RET_CHUNK, (ci_ + 1) * RET_CHUNK)
            cols = slice(p * LANES, (p + 1) * LANES)
            cs = cos_tab[rows]
            sn = sin_tab[rows]
            q_c = z_cols(o_q + p * LANES, LANES, ci_ * RET_CHUNK, RET_CHUNK)
            k_c = z_cols(o_k + p * LANES, LANES, ci_ * RET_CHUNK, RET_CHUNK)
            v_c = z_cols(o_v + p * LANES, LANES, ci_ * RET_CHUNK, RET_CHUNK)
            qr = q_c * cs + _swap_halves(q_c, lane_in_head, half) * sn
            kr = (k_c * cs + _swap_halves(k_c, lane_in_head, half) * sn) * inv_sqrt_dh
            k_bd = jnp.concatenate([jnp.where(lo_mask, kr, 0.0), jnp.where(lo_mask, 0.0, kr)], axis=0)
            v_bd = jnp.concatenate([jnp.where(lo_mask, v_c, 0.0), jnp.where(lo_mask, 0.0, v_c)], axis=0)
            scores = lax.dot_general(qr.astype(BF16), k_bd.astype(BF16), (((1,), (1,)), ((), ())),
                                     preferred_element_type=F32)
            scores = scores * dmask_ref[p]
            o_c = jnp.dot(scores.astype(BF16), v_bd.astype(BF16), preferred_element_type=F32)
            o_c = o_c + jnp.dot((qr * qdec_ref[:, cols]).astype(BF16), ret_state[p].astype(BF16),
                                preferred_element_type=F32)
            kv = jnp.dot((kr * kdec_ref[:, cols]).T.astype(BF16), v_c.astype(BF16), preferred_element_type=F32)
            ret_state[p] = cdec_ref[:, cols] * ret_state[p] + jnp.where(same_head, kv, 0.0)
            ret_out[p].append(o_c)

    n_tail = CONV_WIDTH - 1
    uc_rows = []
    for k in range(n_blk):
        early = {}
        for dd in range(1, n_tail + 1):
            cur = u_rows[k][SUBLANES - dd]
            if k == 0:
                prev = tail_ref[(n_tail - dd) * SUBLANES:(n_tail - dd + 1) * SUBLANES, :]
            else:
                prev = u_rows[k - 1][SUBLANES - dd]
            early[-dd] = pltpu.roll(jnp.where(sub_l == SUBLANES - 1, prev, cur), 1, 0)
        for b in range(SUBLANES):
            acc = conv_b_ref[...]
            for kk in range(CONV_WIDTH):
                src = b - n_tail + kk
                acc = acc + (u_rows[k][src] if src >= 0 else early[src]) * conv_w_ref[kk:kk + 1, :]
            uc_rows.append(acc)
    for dd in range(1, n_tail + 1):
        tail_ref[(n_tail - dd) * SUBLANES:(n_tail - dd + 1) * SUBLANES, :] = u_rows[n_blk - 1][SUBLANES - dd]
    project(1)
    uc = jnp.concatenate(uc_rows, axis=0)
    uc_bf = uc.astype(BF16)
    r_parts, i_parts = [], []
    for c0 in range(0, d_lru, MXU_DIM):
        wid = min(MXU_DIM, d_lru - c0)
        gz = (jnp.dot(uc_bf[:, c0:c0 + wid], wg_ref[c0:c0 + wid, 2 * c0:2 * (c0 + wid)],
                      preferred_element_type=F32) + bg_ref[:, 2 * c0:2 * (c0 + wid)])
        r_parts.append(gz[:, 0:wid])
        i_parts.append(gz[:, wid:2 * wid])
    r_gate = jax.nn.sigmoid(jnp.concatenate(r_parts, axis=1))
    i_gate = jax.nn.sigmoid(jnp.concatenate(i_parts, axis=1))
    project(1)
    lam = lam_ref[...]
    softplus_neg_lam = jnp.maximum(-lam, 0.0) + jnp.log(1.0 + jnp.exp(-jnp.abs(lam)))
    a_coef = jnp.exp((-LRU_C * softplus_neg_lam) * r_gate)
    b_term = jnp.sqrt(1.0 - a_coef * a_coef) * (i_gate * uc)
    project(1)

    h_car = jnp.broadcast_to(hcar_ref[...], (SUBLANES, d_lru))
    for k in range(n_blk):
        if k % 4 == 0:
            project(1)
        h_prev = jnp.where(sub_l == 0, h_car, 0.0)
        h_loc, a_cum = [], []
        for b in range(SUBLANES):
            a_b = a_coef[blk_rows(k, b)]
            h_prev = a_b * h_prev + b_term[blk_rows(k, b)]
            h_loc.append(h_prev)
            a_cum.append(a_b if b == 0 else a_b * a_cum[-1])
        pa, pb = a_cum[-1], h_loc[-1]
        for d in _HS_SHIFTS:
            live = sub_l >= d
            pb = jnp.where(live, pa * pltpu.roll(pb, d, 0) + pb, pb)
            pa = jnp.where(live, pa * pltpu.roll(pa, d, 0), pa)
        h_in = jnp.where(sub_l == 0, 0.0, pltpu.roll(pb, 1, 0))
        h_car = _row_bcast(pb, SUBLANES - 1)
        for b in range(SUBLANES):
            h_t = h_loc[b] + a_cum[b] * h_in
            for i in range(lru_slabs):
                ycat_ref[i, scan_rows(k, b), :] = h_t[:, i * LANES:(i + 1) * LANES]
    hcar_ref[...] = h_car[0:1, :]

    us_p = jnp.concatenate([us_rows[k][b] for k in range(n_blk) for b in range(SUBLANES)], axis=0)
    bu_ref[...] = jnp.dot(us_p.astype(BF16), bbar_ref[...], preferred_element_type=F32)
    re_cols = slice(0, n_state)
    im_cols = slice(n_state, 2 * n_state)
    zr = ztab_ref[_ZT_STEP_RE]
    zi = ztab_ref[_ZT_STEP_IM]
    xc_r = jnp.broadcast_to(xcar_ref[0:1, :], (SUBLANES, n_state))
    xc_i = jnp.broadcast_to(xcar_ref[1:2, :], (SUBLANES, n_state))
    for k in range(n_blk):
        if k % 3 == 0:
            project(1)
        retention_units(2 - k % 2)
        xr = jnp.where(sub_s == 0, xc_r, 0.0)
        xi = jnp.where(sub_s == 0, xc_i, 0.0)
        for b in range(SUBLANES):
            xr, xi = (zr * xr - zi * xi + bu_ref[blk_rows(k, b), re_cols],
                      zr * xi + zi * xr + bu_ref[blk_rows(k, b), im_cols])
            bu_ref[blk_rows(k, b), re_cols] = xr
            bu_ref[blk_rows(k, b), im_cols] = xi
        pr, pi = xr, xi
        for n, d in enumerate(_HS_SHIFTS):
            hr, hi = ztab_ref[_ZT_HS_RE + n], ztab_ref[_ZT_HS_IM + n]
            sr, si = pltpu.roll(pr, d, 0), pltpu.roll(pi, d, 0)
            pr, pi = pr + hr * sr - hi * si, pi + hr * si + hi * sr
        xin_r = jnp.where(sub_s == 0, 0.0, pltpu.roll(pr, 1, 0))
        xin_i = jnp.where(sub_s == 0, 0.0, pltpu.roll(pi, 1, 0))
        xc_r = _row_bcast(pr, SUBLANES - 1)
        xc_i = _row_bcast(pi, SUBLANES - 1)
        for b0 in range(0, SUBLANES, 2):
            halves_r, halves_i = [], []
            for b in (b0, b0 + 1):
                sr, si = ztab_ref[_ZT_STEP_RE + b], ztab_ref[_ZT_STEP_IM + b]
                halves_r.append(bu_ref[blk_rows(k, b), re_cols] + sr * xin_r - si * xin_i)
                halves_i.append(bu_ref[blk_rows(k, b), im_cols] + sr * xin_i + si * xin_r)
            rows2 = slice(k * SCAN_BLOCK + b0 * SUBLANES, k * SCAN_BLOCK + (b0 + 2) * SUBLANES)
            xb_ref[rows2, re_cols] = jnp.concatenate(halves_r, axis=0).astype(BF16)
            xb_ref[rows2, im_cols] = jnp.concatenate(halves_i, axis=0).astype(BF16)
    xcar_ref[0:1, :] = xc_r[0:1, :]
    xcar_ref[1:2, :] = xc_i[0:1, :]

    y_s = jnp.dot(xb_ref[...], s5_c_ref[...], preferred_element_type=F32)
    y_s = jax.nn.gelu(y_s + s5_d_ref[...] * us_p)
    glu = jnp.dot(y_s.astype(BF16), wglu_ref[...], preferred_element_type=F32) + bglu_ref[...]
    y_ssm = y_s * jax.nn.sigmoid(glu)
    for k in range(n_blk):
        for b in range(SUBLANES):
            for i in range(ssm_slabs):
                ycat_ref[lru_slabs + i, scan_rows(k, b), :] = y_ssm[blk_rows(k, b), i * LANES:(i + 1) * LANES]

    retention_units(len(ret_units))
    for p in range(ret_pairs):
        rstate_ref[p] = ret_state[p]
    o_all = jnp.concatenate([jnp.concatenate(ret_out[p], axis=0) for p in range(ret_pairs)], axis=1)
    def head_mean(v):
        parts = []
        for c0 in range(0, d_ret, MXU_DIM):
            c1 = min(c0 + MXU_DIM, d_ret)
            parts.append(jnp.dot(v[:, c0:c1].astype(BF16), avg_ref[c0:c1, c0:c1], preferred_element_type=F32))
        return jnp.concatenate(parts, axis=1)

    mu = head_mean(o_all)
    dev = o_all - mu
    d2 = dev * dev
    var = head_mean(d2)
    o_n = dev * lax.rsqrt(var + LN_EPS) * gn_g_ref[...] + gn_b_ref[...]
    y_ret = _silu(z_cols(o_gret, d_ret)) * o_n

    y_lru = [ycat_ref[i] * jax.nn.gelu(z_cols(o_glru + i * LANES, LANES))
             for i in range(lru_slabs)]
    y_ssm_t = [ycat_ref[lru_slabs + i] for i in range(ssm_slabs)]
    ycat = jnp.concatenate([v.astype(BF16) for v in y_lru] + [y_ret.astype(BF16)]
                           + [v.astype(BF16) for v in y_ssm_t], axis=1)
    m = jnp.dot(ycat, w_out_ref[...], preferred_element_type=F32)
    y = alpha * x + gate * m
    n_ln_blocks = max(1, len(proj_chunks))
    ln_rows = tile // n_ln_blocks
    for r in range(n_ln_blocks):
        rows = slice(r * ln_rows, (r + 1) * ln_rows)
        o_ref[0, rows, :] = _layer_norm_rows(y[rows], lng_ref[...], lnb_ref[...])
        project(1)


def _mixer_call(x, pos3, mod, prm, ln_g, ln_b, *, layer, alpha, tile):
    bsz, seq, d = x.shape
    depth = prm["lam"].shape[0]
    d_lru = prm["lam"].shape[2]
    d_ret = prm["gn_g"].shape[2]
    d_ssm = prm["s5_d"].shape[2]
    n_state = prm["s5_lr"].shape[2]
    n_in = prm["w_in"].shape[2]
    n_scan_slabs = (d_lru + d_ssm) // LANES
    tiles_per_seq = seq // tile
    n_tiles = bsz * tiles_per_seq
    mod_rows = mod.shape[0] // (depth * 3)
    names = ["w_in", "conv_w", "conv_b", "wg", "bg", "lam", "lg", "gn_g", "gn_b", "avg",
             "s5_lr", "s5_li", "s5_dt", "s5_bre", "s5_bim", "s5_c", "s5_d", "wglu", "bglu", "w_out"]
    consts = [prm[k] for k in names]
    scratch = [
        pltpu.VMEM((tile, n_in), F32),
        pltpu.VMEM((n_scan_slabs, tile, LANES), F32),
        pltpu.VMEM((tile, n_in), F32),
        pltpu.VMEM((n_scan_slabs, tile, LANES), F32),
        pltpu.VMEM((n_scan_slabs, tile, LANES), F32),
        pltpu.VMEM(((CONV_WIDTH - 1) * SUBLANES, d_lru), F32),
        pltpu.VMEM((1, d_lru), F32),
        pltpu.VMEM((tile, 2 * n_state), F32),
        pltpu.VMEM((tile, 2 * n_state), BF16),
        pltpu.VMEM((2, n_state), F32),
        pltpu.VMEM((_ZT_ROWS, SUBLANES, n_state), F32),
        pltpu.VMEM((d_ssm, 2 * n_state), BF16),
        pltpu.VMEM((d_ret // LANES, RET_CHUNK, 2 * RET_CHUNK), F32),
        pltpu.VMEM((RET_CHUNK, d_ret), F32),
        pltpu.VMEM((RET_CHUNK, d_ret), F32),
        pltpu.VMEM((1, d_ret), F32),
        pltpu.VMEM((d_ret // LANES, LANES, LANES), F32),
    ]

    def proj_tile(g):
        return jnp.minimum(g, n_tiles - 1)

    def mix_tile(g):
        return jnp.maximum(g - 1, 0)

    def x_index(t):
        return (t // tiles_per_seq, t % tiles_per_seq, 0)

    def mod_index(t):
        return ((layer * mod_rows + t // tiles_per_seq) * 3 + 1, 0, 0)

    return pl.pallas_call(
        functools.partial(_mixer_kernel, alpha=alpha, tile=tile, tiles_per_seq=tiles_per_seq,
                          d_lru=d_lru, d_ret=d_ret, d_ssm=d_ssm),
        grid=(n_tiles + 1,),
        in_specs=[
            pl.BlockSpec((1, tile, d), lambda g: x_index(proj_tile(g))),
            pl.BlockSpec((None, 3, d), lambda g: mod_index(proj_tile(g))),
            pl.BlockSpec((1, tile, d), lambda g: x_index(mix_tile(g))),
            pl.BlockSpec((1, 1, tile), lambda g: (mix_tile(g), 0, 0)),
            pl.BlockSpec((None, 3, d), lambda g: mod_index(mix_tile(g))),
        ] + [_layer_spec(a.shape, layer) for a in consts]
          + [_layer_spec(ln_g.shape, layer * 3 + 1), _layer_spec(ln_b.shape, layer * 3 + 1)],
        out_specs=pl.BlockSpec((1, tile, d), lambda g: x_index(mix_tile(g))),
        out_shape=jax.ShapeDtypeStruct(x.shape, F32),
        scratch_shapes=scratch,
        compiler_params=pltpu.CompilerParams(
            dimension_semantics=("arbitrary",),
            vmem_limit_bytes=VMEM_LIMIT_BYTES),
    )(x, mod, x, pos3, mod, *consts, ln_g, ln_b)


def _block_diag(blocks):
    depth, n, r, c = blocks.shape
    eye = jnp.eye(n, dtype=blocks.dtype)
    return (eye[None, :, None, :, None] * blocks[:, :, :, None, :]).reshape(depth, n * r, n * c)


def _gate_groups(a, b):
    n = a.shape[-1]
    parts = []
    for c0 in range(0, n, MXU_DIM):
        parts += [a[..., c0:c0 + MXU_DIM], b[..., c0:c0 + MXU_DIM]]
    return jnp.concatenate(parts, axis=-1)


def _mixer_params(mix_w_in, conv_w, conv_b, lru_wa, lru_ba, lru_wx, lru_bx, lru_lam,
                  ret_gn_g, ret_gn_b, ssm_lam_re, ssm_lam_im, ssm_log_step, ssm_b_re, ssm_b_im,
                  ssm_c_re, ssm_c_im, ssm_d, ssm_w_glu, ssm_b_glu, mix_w_out):
    depth, d_lru = lru_lam.shape
    d_ret = ret_gn_g.shape[1]
    d_ssm = ssm_d.shape[1]
    groups, n_per = ssm_lam_re.shape[1], ssm_lam_re.shape[2]
    n_state = groups * n_per
    head_dim = d_ret // RET_HEADS
    log_gamma = np.array([math.log1p(-2.0 ** (-5.0 - hh)) for hh in range(RET_HEADS)], np.float64)
    lg_lane = np.repeat(log_gamma, head_dim)[None, None, :].repeat(depth, axis=0)
    avg = np.kron(np.eye(RET_HEADS), np.full((head_dim, head_dim), 1.0 / head_dim))[None].repeat(depth, axis=0)
    c_re_bd = _block_diag(jnp.swapaxes(ssm_c_re, 2, 3))
    c_im_bd = _block_diag(jnp.swapaxes(ssm_c_im, 2, 3))
    return {
        "w_in": mix_w_in.astype(BF16),
        "conv_w": conv_w,
        "conv_b": conv_b.reshape(depth, 1, d_lru),
        "wg": _gate_groups(_block_diag(lru_wa), _block_diag(lru_wx)).astype(BF16),
        "bg": _gate_groups(lru_ba[:, None, :], lru_bx[:, None, :]),
        "lam": lru_lam.reshape(depth, 1, d_lru),
        "lg": jnp.asarray(lg_lane, F32),
        "gn_g": ret_gn_g.reshape(depth, 1, d_ret),
        "gn_b": ret_gn_b.reshape(depth, 1, d_ret),
        "avg": jnp.asarray(avg, BF16),
        "s5_lr": ssm_lam_re.reshape(depth, 1, n_state),
        "s5_li": ssm_lam_im.reshape(depth, 1, n_state),
        "s5_dt": jnp.repeat(ssm_log_step, n_per, axis=1).reshape(depth, 1, n_state),
        "s5_bre": _block_diag(jnp.swapaxes(ssm_b_re, 2, 3)),
        "s5_bim": _block_diag(jnp.swapaxes(ssm_b_im, 2, 3)),
        "s5_c": jnp.concatenate([c_re_bd, -c_im_bd], axis=1).astype(BF16),
        "s5_d": ssm_d.reshape(depth, 1, d_ssm),
        "wglu": ssm_w_glu.astype(BF16),
        "bglu": ssm_b_glu.reshape(depth, 1, d_ssm),
        "w_out": mix_w_out.astype(BF16),
    }


def _tiles(seq):
    tm = 512 if seq % 512 == 0 else seq
    tile = 512 if seq % 512 == 0 else seq
    return tm, tile


def kernel(x, c, positions, ada_w, ada_b, ln_g, ln_b, ffn1_w1, ffn1_w3, ffn1_w2, mix_w_in, conv_w, conv_b, lru_wa, lru_ba, lru_wx, lru_bx, lru_lam, ret_gn_g, ret_gn_b, ssm_lam_re, ssm_lam_im, ssm_log_step, ssm_b_re, ssm_b_im, ssm_c_re, ssm_c_im, ssm_d, ssm_w_glu, ssm_b_glu, mix_w_out, ffn2_w1, ffn2_w3, ffn2_w2):
    bsz, seq, d = x.shape
    depth = ada_w.shape[0]
    alpha = (2.0 * depth) ** 0.25
    tm, tile = _tiles(seq)
    f_chunk = 256

    rows = -(-bsz // SUBLANES) * SUBLANES
    c_pad = jnp.pad(c, ((0, rows - bsz), (0, 0)))
    mod = _ada_call(c_pad, ada_w, ada_b).reshape(depth * rows * 3, 3, d)
    pos3 = positions.reshape(bsz * (seq // tile), 1, tile)
    lng = ln_g.reshape(depth * 3, 1, d)
    lnb = ln_b.reshape(depth * 3, 1, d)
    ffn1 = (ffn1_w1.astype(BF16), ffn1_w3.astype(BF16), ffn1_w2.astype(BF16))
    ffn2 = (ffn2_w1.astype(BF16), ffn2_w3.astype(BF16), ffn2_w2.astype(BF16))
    prm = _mixer_params(mix_w_in, conv_w, conv_b, lru_wa, lru_ba, lru_wx, lru_bx, lru_lam,
                        ret_gn_g, ret_gn_b, ssm_lam_re, ssm_lam_im, ssm_log_step, ssm_b_re, ssm_b_im,
                        ssm_c_re, ssm_c_im, ssm_d, ssm_w_glu, ssm_b_glu, mix_w_out)

    for l in range(depth):
        x = _ffn_call(x, mod, *ffn1, lng, lnb, layer=l, sub=0, alpha=alpha, tm=tm, f_chunk=f_chunk)
        x = _mixer_call(x, pos3, mod, prm, lng, lnb, layer=l, alpha=alpha, tile=tile)
        x = _ffn_call(x, mod, *ffn2, lng, lnb, layer=l, sub=2, alpha=alpha, tm=tm, f_chunk=f_chunk)
    return x
```

```python
import functools
import math

import numpy as np
import jax
import jax.numpy as jnp
from jax import lax
from jax.experimental import pallas as pl
from jax.experimental.pallas import tpu as pltpu

F32 = jnp.float32
BF16 = jnp.bfloat16

LRU_HEADS = 6
CONV_WIDTH = 4
LRU_C = 8.0
RET_HEADS = 6
RET_CHUNK = 128
ROPE_BASE = 10000.0
SSM_GROUP = 16
SSM_STATE = 64
N_MOD = 9
MACARON_HALF = 0.5
LN_EPS = 1e-5

LANES = 128
SUBLANES = 8
VMEM_LIMIT_BYTES = 56 * 1024 * 1024
SCAN_BLOCK = SUBLANES * SUBLANES
MXU_DIM = 256
PROJ_CHUNK = MXU_DIM


def _silu(v):
    return v * jax.nn.sigmoid(v)


def _layer_norm_rows(v, g, b):
    mu = jnp.mean(v, axis=-1, keepdims=True)
    d = v - mu
    var = jnp.mean(d * d, axis=-1, keepdims=True)
    return d * lax.rsqrt(var + LN_EPS) * g + b


def _layer_spec(shape, layer):
    tail = (0,) * (len(shape) - 1)
    return pl.BlockSpec((None,) + tuple(shape[1:]), lambda *_: (layer,) + tail,
                        pipeline_mode=pl.Buffered(1))


def _ada_kernel(c_ref, w_ref, b_ref, o_ref):
    cond = _silu(c_ref[...]).astype(BF16)
    o_ref[0] = jnp.dot(cond, w_ref[0].astype(BF16), preferred_element_type=F32) + b_ref[0]


def _ada_call(c_pad, ada_w, ada_b):
    depth, d, n = ada_w.shape
    rows = c_pad.shape[0]
    tn = n // N_MOD
    return pl.pallas_call(
        _ada_kernel,
        grid=(depth, n // tn),
        in_specs=[
            pl.BlockSpec((rows, d), lambda l, j: (0, 0)),
            pl.BlockSpec((1, d, tn), lambda l, j: (l, 0, j)),
            pl.BlockSpec((1, 1, tn), lambda l, j: (l, 0, j)),
        ],
        out_specs=pl.BlockSpec((1, rows, tn), lambda l, j: (l, 0, j)),
        out_shape=jax.ShapeDtypeStruct((depth, rows, n), F32),
        compiler_params=pltpu.CompilerParams(
            dimension_semantics=("arbitrary", "arbitrary"),
            vmem_limit_bytes=VMEM_LIMIT_BYTES),
    )(c_pad, ada_w, ada_b.reshape(depth, 1, n))


def _ffn_kernel(x_ref, mod_ref, w1_ref, w3_ref, w2_ref, lng_ref, lnb_ref, o_ref, y_ref,
                *, alpha, f_chunk, n_tiles):
    tm = x_ref.shape[1]
    d_ff = w1_ref.shape[1]
    f_starts = list(range(0, d_ff, f_chunk))
    ln_rows = tm // SUBLANES

    def step(do_matmul, do_norm):
        ln_blocks = list(range(0, tm, ln_rows)) if do_norm else []
        anchor = {"from_mm": None, "from_ln": None}

        def norm_rows():
            if ln_blocks:
                r0 = ln_blocks.pop(0)
                y_blk = y_ref[r0:r0 + ln_rows, :]
                if anchor["from_mm"] is not None:
                    y_blk = y_blk + anchor["from_mm"]
                out = _layer_norm_rows(y_blk, lng_ref[...], lnb_ref[...])
                o_ref[0, r0:r0 + ln_rows, :] = out
                anchor["from_ln"] = out[0:1, :] * 0.0

        if do_matmul:
            x = x_ref[0]
            shift = mod_ref[0:1, :]
            scale = mod_ref[1:2, :]
            gate = mod_ref[2:3, :]
            h = (x * (1.0 + scale) + shift).astype(BF16)
            acc = jnp.zeros(x.shape, F32)
            for f0 in f_starts:
                h_c = h
                if anchor["from_ln"] is not None:
                    top = h[0:2 * SUBLANES, :] + anchor["from_ln"].astype(BF16)
                    h_c = jnp.concatenate([top, h[2 * SUBLANES:, :]], axis=0)
                a = jnp.dot(h_c, w1_ref[:, f0:f0 + f_chunk], preferred_element_type=F32)
                b = jnp.dot(h_c, w3_ref[:, f0:f0 + f_chunk], preferred_element_type=F32)
                g = (_silu(a) * b).astype(BF16)
                norm_rows()
                acc = acc + jnp.dot(g, w2_ref[f0:f0 + f_chunk, :], preferred_element_type=F32)
                if do_norm:
                    anchor["from_mm"] = a[0:1, 0:1] * 0.0
        while ln_blocks:
            norm_rows()
        if do_matmul:
            y_ref[...] = alpha * x + (MACARON_HALF * gate) * acc

    g_idx = pl.program_id(0)
    pl.when(g_idx == 0)(functools.partial(step, True, False))
    pl.when(jnp.logical_and(g_idx > 0, g_idx < n_tiles))(functools.partial(step, True, True))
    pl.when(g_idx == n_tiles)(functools.partial(step, False, True))


def _ffn_call(x, mod, w1, w3, w2, ln_g, ln_b, *, layer, sub, alpha, tm, f_chunk):
    bsz, seq, d = x.shape
    mod_rows = mod.shape[0] // (w1.shape[0] * 3)
    tiles_per_seq = seq // tm
    n_tiles = bsz * tiles_per_seq

    def mm_tile(g):
        return jnp.minimum(g, n_tiles - 1)

    def ln_tile(g):
        return jnp.maximum(g - 1, 0)

    def x_index(t):
        return (t // tiles_per_seq, t % tiles_per_seq, 0)

    return pl.pallas_call(
        functools.partial(_ffn_kernel, alpha=alpha, f_chunk=f_chunk, n_tiles=n_tiles),
        grid=(n_tiles + 1,),
        in_specs=[
            pl.BlockSpec((1, tm, d), lambda g: x_index(mm_tile(g))),
            pl.BlockSpec((None, 3, d), lambda g: ((layer * mod_rows + mm_tile(g) // tiles_per_seq) * 3 + sub, 0, 0)),
            _layer_spec(w1.shape, layer),
            _layer_spec(w3.shape, layer),
            _layer_spec(w2.shape, layer),
            _layer_spec(ln_g.shape, layer * 3 + sub),
            _layer_spec(ln_b.shape, layer * 3 + sub),
        ],
        out_specs=pl.BlockSpec((1, tm, d), lambda g: x_index(ln_tile(g))),
        out_shape=jax.ShapeDtypeStruct(x.shape, F32),
        scratch_shapes=[pltpu.VMEM((tm, d), F32)],
        compiler_params=pltpu.CompilerParams(
            dimension_semantics=("arbitrary",),
            vmem_limit_bytes=VMEM_LIMIT_BYTES),
    )(x, mod, w1, w3, w2, ln_g, ln_b)


_ZT_STEP_RE = 0
_ZT_STEP_IM = SUBLANES
_ZT_HS_RE = 2 * SUBLANES
_ZT_HS_IM = 2 * SUBLANES + 3
_ZT_ROWS = 2 * SUBLANES + 6
_HS_SHIFTS = (1, 2, 4)


def _swap_halves(v, lane_in_head, half):
    n = v.shape[-1]
    fwd = pltpu.roll(v, half, 1)
    bwd = pltpu.roll(v, n - half, 1)
    return jnp.where(lane_in_head < half, bwd, fwd)


def _row_bcast(v, row):
    return jnp.broadcast_to(v[row:row + 1, :], v.shape)


_N_MIXER_IO = 28


def _mixer_kernel(*refs, **static):
    io, (z_a, zs_a, z_b, zs_b), rest = refs[:_N_MIXER_IO], refs[_N_MIXER_IO:_N_MIXER_IO + 4], refs[_N_MIXER_IO + 4:]
    parity = pl.program_id(0) % 2

    @pl.when(parity == 0)
    def _even_step():
        _mixer_step(*io, z_a, zs_a, z_b, zs_b, *rest, **static)

    @pl.when(parity == 1)
    def _odd_step():
        _mixer_step(*io, z_b, zs_b, z_a, zs_a, *rest, **static)


def _mixer_step(
        xp_ref, modp_ref,
        x_ref, pos_ref, mod_ref, w_in_ref, conv_w_ref, conv_b_ref, wg_ref, bg_ref, lam_ref,
        lg_ref, gn_g_ref, gn_b_ref, avg_ref,
        s5_lr_ref, s5_li_ref, s5_dt_ref, s5_bre_ref, s5_bim_ref, s5_c_ref, s5_d_ref,
        wglu_ref, bglu_ref, w_out_ref, lng_ref, lnb_ref,
        o_ref,
        zp_ref, zsp_ref, zc_ref, zsc_ref,
        ycat_ref, tail_ref, hcar_ref, bu_ref, xb_ref, xcar_ref, ztab_ref, bbar_ref,
        dmask_ref, qdec_ref, kdec_ref, cdec_ref, rstate_ref,
        *, alpha, tile, tiles_per_seq, d_lru, d_ret, d_ssm):
    g_idx = pl.program_id(0)
    first_call_step = g_idx == 0
    n_state = bu_ref.shape[1] // 2
    n_blk = tile // SCAN_BLOCK
    lru_slabs = d_lru // LANES
    ssm_slabs = d_ssm // LANES
    ret_pairs = d_ret // LANES
    head_dim = d_ret // RET_HEADS
    half = head_dim // 2

    @pl.when(first_call_step)
    def _init_tables():
        lr = s5_lr_ref[...]
        li = s5_li_ref[...]
        dt = jnp.exp(s5_dt_ref[...])
        sub = lax.broadcasted_iota(jnp.int32, (SUBLANES, n_state), 0)

        def z_power(n):
            mag = jnp.exp(n * (lr * dt))
            ang = n * (li * dt)
            return mag * jnp.cos(ang), mag * jnp.sin(ang)

        st_r, st_i = z_power((sub + 1).astype(F32))
        ch_r, ch_i = z_power((SUBLANES * (sub + 1)).astype(F32))
        for b in range(SUBLANES):
            ztab_ref[_ZT_STEP_RE + b] = _row_bcast(st_r, b)
            ztab_ref[_ZT_STEP_IM + b] = _row_bcast(st_i, b)
        for n, d in enumerate(_HS_SHIFTS):
            ztab_ref[_ZT_HS_RE + n] = jnp.where(sub >= d, _row_bcast(ch_r, d - 1), 0.0)
            ztab_ref[_ZT_HS_IM + n] = jnp.where(sub >= d, _row_bcast(ch_i, d - 1), 0.0)
        zr = st_r[0:1, :]
        zi = st_i[0:1, :]
        den = lr * lr + li * li
        er = ((zr - 1.0) * lr + zi * li) / den
        ei = (zi * lr - (zr - 1.0) * li) / den
        bre = s5_bre_ref[...]
        bim = s5_bim_ref[...]
        bbar_ref[:, 0:n_state] = (er * bre - ei * bim).astype(BF16)
        bbar_ref[:, n_state:2 * n_state] = (er * bim + ei * bre).astype(BF16)
        tq = lax.broadcasted_iota(jnp.int32, (RET_CHUNK, 2 * RET_CHUNK), 0)
        tk = lax.broadcasted_iota(jnp.int32, (RET_CHUNK, 2 * RET_CHUNK), 1) % RET_CHUNK
        diff = (tq - tk).astype(F32)
        row = lax.broadcasted_iota(jnp.int32, (RET_CHUNK, d_ret), 0).astype(F32)
        lg_row = lg_ref[...]
        qdec_ref[...] = jnp.exp(lg_row * (row + 1.0))
        kdec_ref[...] = jnp.exp(lg_row * (RET_CHUNK - 1.0 - row))
        cdec_ref[...] = jnp.exp(lg_row * float(RET_CHUNK))
        for p in range(ret_pairs):
            lg_lo = lg_ref[0:1, p * LANES:p * LANES + 1]
            lg_hi = lg_ref[0:1, p * LANES + head_dim:p * LANES + head_dim + 1]
            col = lax.broadcasted_iota(jnp.int32, (RET_CHUNK, 2 * RET_CHUNK), 1)
            lg_pair = jnp.where(col < RET_CHUNK, lg_lo, lg_hi)
            dmask_ref[p] = jnp.where(diff >= 0.0, jnp.exp(lg_pair * jnp.maximum(diff, 0.0)), 0.0)

    @pl.when(jnp.logical_or(g_idx == 0, (g_idx - 1) % tiles_per_seq == 0))
    def _reset_state():
        tail_ref[...] = jnp.zeros(tail_ref.shape, F32)
        hcar_ref[...] = jnp.zeros(hcar_ref.shape, F32)
        xcar_ref[...] = jnp.zeros(xcar_ref.shape, F32)
        rstate_ref[...] = jnp.zeros(rstate_ref.shape, F32)

    n_scan_slabs = lru_slabs + ssm_slabs

    @pl.when(first_call_step)
    def _zero_placeholder_tile():
        zc_ref[...] = jnp.zeros(zc_ref.shape, F32)
        zsc_ref[...] = jnp.zeros(zsc_ref.shape, F32)

    o_lru, o_glru = 0, d_lru
    o_q, o_k, o_v, o_gret = 2 * d_lru, 2 * d_lru + d_ret, 2 * d_lru + 2 * d_ret, 2 * d_lru + 3 * d_ret
    o_ssm = 2 * d_lru + 4 * d_ret

    xp = xp_ref[0]
    hp = (xp * (1.0 + modp_ref[1:2, :]) + modp_ref[0:1, :]).astype(BF16)
    scan_cols = ([o_lru + i * LANES for i in range(lru_slabs)]
                 + [o_ssm + i * LANES for i in range(ssm_slabs)])
    proj_chunks = list(range(0, w_in_ref.shape[1], PROJ_CHUNK))

    def project(n_chunks):
        for _ in range(min(n_chunks, len(proj_chunks))):
            c0 = proj_chunks.pop(0)
            zchunk = jnp.dot(hp, w_in_ref[:, c0:c0 + PROJ_CHUNK], preferred_element_type=F32)
            zp_ref[:, c0:c0 + PROJ_CHUNK] = zchunk
            for s_idx, sc in enumerate(scan_cols):
                if c0 <= sc < c0 + PROJ_CHUNK:
                    zsp_ref[s_idx] = zchunk[:, sc - c0:sc - c0 + LANES]

    x = x_ref[0]
    gate = mod_ref[2:3, :]

    def z_cols(c0, width, r0=0, n_rows=tile):
        return zc_ref[r0:r0 + n_rows, c0:c0 + width]

    def scan_rows(k, b):
        return pl.ds(k * SCAN_BLOCK + b, SUBLANES, stride=SUBLANES)

    def blk_rows(k, b):
        return slice(k * SCAN_BLOCK + b * SUBLANES, k * SCAN_BLOCK + (b + 1) * SUBLANES)

    u_rows = [[jnp.concatenate([zsc_ref[i, scan_rows(k, b), :] for i in range(lru_slabs)], axis=1)
               for b in range(SUBLANES)] for k in range(n_blk)]
    us_rows = [[jnp.concatenate([zsc_ref[lru_slabs + i, scan_rows(k, b), :] for i in range(ssm_slabs)], axis=1)
                for b in range(SUBLANES)] for k in range(n_blk)]

    sub_l = lax.broadcasted_iota(jnp.int32, (SUBLANES, d_lru), 0)
    sub_s = lax.broadcasted_iota(jnp.int32, (SUBLANES, n_state), 0)

    pos_row = pos_ref[0].astype(F32)
    freq = lax.broadcasted_iota(jnp.int32, (half, 1), 0).astype(F32)
    inv = jnp.exp(freq * (-math.log(ROPE_BASE) / half))
    ang_t = inv * pos_row
    cos_t = jnp.cos(ang_t)
    sin_t = jnp.sin(ang_t)
    reps = LANES // head_dim
    cos_tab = jnp.concatenate([cos_t, cos_t] * reps, axis=0).T
    sin_tab = jnp.concatenate([-sin_t, sin_t] * reps, axis=0).T
    lane_in_head = lax.broadcasted_iota(jnp.int32, (RET_CHUNK, LANES), 1) % head_dim
    lane = lax.broadcasted_iota(jnp.int32, (RET_CHUNK, LANES), 1)
    lo_mask = lane < head_dim
    blk_r = lax.broadcasted_iota(jnp.int32, (LANES, LANES), 0) // head_dim
    blk_c = lax.broadcasted_iota(jnp.int32, (LANES, LANES), 1) // head_dim
    same_head = blk_r == blk_c
    inv_sqrt_dh = head_dim ** -0.5
    ret_state = [rstate_ref[p] for p in range(ret_pairs)]
    ret_out = [[] for _ in range(ret_pairs)]
    ret_units = [(ci_, p) for ci_ in range(tile // RET_CHUNK) for p in range(ret_pairs)]

    def retention_units(n_units):
        for _ in range(min(n_units, len(ret_units))):
            ci_, p = ret_units.pop(0)
            rows = slice(ci_ * RET_CHUNK, (ci_ + 1) * RET_CHUNK)
            cols = slice(p * LANES, (p + 1) * LANES)
            cs = cos_tab[rows]
            sn = sin_tab[rows]
            q_c = z_cols(o_q + p * LANES, LANES, ci_ * RET_CHUNK, RET_CHUNK)
            k_c = z_cols(o_k + p * LANES, LANES, ci_ * RET_CHUNK, RET_CHUNK)
            v_c = z_cols(o_v + p * LANES, LANES, ci_ * RET_CHUNK, RET_CHUNK)
            qr = q_c * cs + _swap_halves(q_c, lane_in_head, half) * sn
            kr = (k_c * cs + _swap_halves(k_c, lane_in_head, half) * sn) * inv_sqrt_dh
            k_bd = jnp.concatenate([jnp.where(lo_mask, kr, 0.0), jnp.where(lo_mask, 0.0, kr)], axis=0)
            v_bd = jnp.concatenate([jnp.where(lo_mask, v_c, 0.0), jnp.where(lo_mask, 0.0, v_c)], axis=0)
            scores = lax.dot_general(qr.astype(BF16), k_bd.astype(BF16), (((1,), (1,)), ((), ())),
                                     preferred_element_type=F32)
            scores = scores * dmask_ref[p]
            o_c = jnp.dot(scores.astype(BF16), v_bd.astype(BF16), preferred_element_type=F32)
            o_c = o_c + jnp.dot((qr * qdec_ref[:, cols]).astype(BF16), ret_state[p].astype(BF16),
                                preferred_element_type=F32)
            kv = jnp.dot((kr * kdec_ref[:, cols]).T.astype(BF16), v_c.astype(BF16), preferred_element_type=F32)
            ret_state[p] = cdec_ref[:, cols] * ret_state[p] + jnp.where(same_head, kv, 0.0)
            ret_out[p].append(o_c)

    n_tail = CONV_WIDTH - 1
    uc_rows = []
    for k in range(n_blk):
        early = {}
        for dd in range(1, n_tail + 1):
            cur = u_rows[k][SUBLANES - dd]
            if k == 0:
                prev = tail_ref[(n_tail - dd) * SUBLANES:(n_tail - dd + 1) * SUBLANES, :]
            else:
                prev = u_rows[k - 1][SUBLANES - dd]
            early[-dd] = pltpu.roll(jnp.where(sub_l == SUBLANES - 1, prev, cur), 1, 0)
        for b in range(SUBLANES):
            acc = conv_b_ref[...]
            for kk in range(CONV_WIDTH):
                src = b - n_tail + kk
                acc = acc + (u_rows[k][src] if src >= 0 else early[src]) * conv_w_ref[kk:kk + 1, :]
            uc_rows.append(acc)
    for dd in range(1, n_tail + 1):
        tail_ref[(n_tail - dd) * SUBLANES:(n_tail - dd + 1) * SUBLANES, :] = u_rows[n_blk - 1][SUBLANES - dd]
    project(1)
    uc = jnp.concatenate(uc_rows, axis=0)
    uc_bf = uc.astype(BF16)
    r_parts, i_parts = [], []
    for c0 in range(0, d_lru, MXU_DIM):
        wid = min(MXU_DIM, d_lru - c0)
        gz = (jnp.dot(uc_bf[:, c0:c0 + wid], wg_ref[c0:c0 + wid, 2 * c0:2 * (c0 + wid)],
                      preferred_element_type=F32) + bg_ref[:, 2 * c0:2 * (c0 + wid)])
        r_parts.append(gz[:, 0:wid])
        i_parts.append(gz[:, wid:2 * wid])
    r_gate = jax.nn.sigmoid(jnp.concatenate(r_parts, axis=1))
    i_gate = jax.nn.sigmoid(jnp.concatenate(i_parts, axis=1))
    project(1)
    lam = lam_ref[...]
    softplus_neg_lam = jnp.maximum(-lam, 0.0) + jnp.log(1.0 + jnp.exp(-jnp.abs(lam)))
    a_coef = jnp.exp((-LRU_C * softplus_neg_lam) * r_gate)
    b_term = jnp.sqrt(1.0 - a_coef * a_coef) * (i_gate * uc)
    project(1)

    h_car = jnp.broadcast_to(hcar_ref[...], (SUBLANES, d_lru))
    for k in range(n_blk):
        if k % 4 == 0:
            project(1)
        h_prev = jnp.where(sub_l == 0, h_car, 0.0)
        h_loc, a_cum = [], []
        for b in range(SUBLANES):
            a_b = a_coef[blk_rows(k, b)]
            h_prev = a_b * h_prev + b_term[blk_rows(k, b)]
            h_loc.append(h_prev)
            a_cum.append(a_b if b == 0 else a_b * a_cum[-1])
        pa, pb = a_cum[-1], h_loc[-1]
        for d in _HS_SHIFTS:
            live = sub_l >= d
            pb = jnp.where(live, pa * pltpu.roll(pb, d, 0) + pb, pb)
            pa = jnp.where(live, pa * pltpu.roll(pa, d, 0), pa)
        h_in = jnp.where(sub_l == 0, 0.0, pltpu.roll(pb, 1, 0))
        h_car = _row_bcast(pb, SUBLANES - 1)
        for b in range(SUBLANES):
            h_t = h_loc[b] + a_cum[b] * h_in
            for i in range(lru_slabs):
                ycat_ref[i, scan_rows(k, b), :] = h_t[:, i * LANES:(i + 1) * LANES]
    hcar_ref[...] = h_car[0:1, :]

    us_p = jnp.concatenate([us_rows[k][b] for k in range(n_blk) for b in range(SUBLANES)], axis=0)
    bu_ref[...] = jnp.dot(us_p.astype(BF16), bbar_ref[...], preferred_element_type=F32)
    re_cols = slice(0, n_state)
    im_cols = slice(n_state, 2 * n_state)
    zr = ztab_ref[_ZT_STEP_RE]
    zi = ztab_ref[_ZT_STEP_IM]
    xc_r = jnp.broadcast_to(xcar_ref[0:1, :], (SUBLANES, n_state))
    xc_i = jnp.broadcast_to(xcar_ref[1:2, :], (SUBLANES, n_state))
    for k in range(n_blk):
        if k % 3 == 0:
            project(1)
        retention_units(2 - k % 2)
        xr = jnp.where(sub_s == 0, xc_r, 0.0)
        xi = jnp.where(sub_s == 0, xc_i, 0.0)
        for b in range(SUBLANES):
            xr, xi = (zr * xr - zi * xi + bu_ref[blk_rows(k, b), re_cols],
                      zr * xi + zi * xr + bu_ref[blk_rows(k, b), im_cols])
            bu_ref[blk_rows(k, b), re_cols] = xr
            bu_ref[blk_rows(k, b), im_cols] = xi
        pr, pi = xr, xi
        for n, d in enumerate(_HS_SHIFTS):
            hr, hi = ztab_ref[_ZT_HS_RE + n], ztab_ref[_ZT_HS_IM + n]
            sr, si = pltpu.roll(pr, d, 0), pltpu.roll(pi, d, 0)
            pr, pi = pr + hr * sr - hi * si, pi + hr * si + hi * sr
        xin_r = jnp.where(sub_s == 0, 0.0, pltpu.roll(pr, 1, 0))
        xin_i = jnp.where(sub_s == 0, 0.0, pltpu.roll(pi, 1, 0))
        xc_r = _row_bcast(pr, SUBLANES - 1)
        xc_i = _row_bcast(pi, SUBLANES - 1)
        for b0 in range(0, SUBLANES, 2):
            halves_r, halves_i = [], []
            for b in (b0, b0 + 1):
                sr, si = ztab_ref[_ZT_STEP_RE + b], ztab_ref[_ZT_STEP_IM + b]
                halves_r.append(bu_ref[blk_rows(k, b), re_cols] + sr * xin_r - si * xin_i)
                halves_i.append(bu_ref[blk_rows(k, b), im_cols] + sr * xin_i + si * xin_r)
            rows2 = slice(k * SCAN_BLOCK + b0 * SUBLANES, k * SCAN_BLOCK + (b0 + 2) * SUBLANES)
            xb_ref[rows2, re_cols] = jnp.concatenate(halves_r, axis=0).astype(BF16)
            xb_ref[rows2, im_cols] = jnp.concatenate(halves_i, axis=0).astype(BF16)
    xcar_ref[0:1, :] = xc_r[0:1, :]
    xcar_ref[1:2, :] = xc_i[0:1, :]

    y_s = jnp.dot(xb_ref[...], s5_c_ref[...], preferred_element_type=F32)
    y_s = jax.nn.gelu(y_s + s5_d_ref[...] * us_p)
    glu = jnp.dot(y_s.astype(BF16), wglu_ref[...], preferred_element_type=F32) + bglu_ref[...]
    y_ssm = y_s * jax.nn.sigmoid(glu)
    for k in range(n_blk):
        for b in range(SUBLANES):
            for i in range(ssm_slabs):
                ycat_ref[lru_slabs + i, scan_rows(k, b), :] = y_ssm[blk_rows(k, b), i * LANES:(i + 1) * LANES]

    retention_units(len(ret_units))
    for p in range(ret_pairs):
        rstate_ref[p] = ret_state[p]
    o_all = jnp.concatenate([jnp.concatenate(ret_out[p], axis=0) for p in range(ret_pairs)], axis=1)
    def head_mean(v):
        parts = []
        for c0 in range(0, d_ret, MXU_DIM):
            c1 = min(c0 + MXU_DIM, d_ret)
            parts.append(jnp.dot(v[:, c0:c1].astype(BF16), avg_ref[c0:c1, c0:c1], preferred_element_type=F32))
        return jnp.concatenate(parts, axis=1)

    mu = head_mean(o_all)
    dev = o_all - mu
    d2 = dev * dev
    var = head_mean(d2)
    o_n = dev * lax.rsqrt(var + LN_EPS) * gn_g_ref[...] + gn_b_ref[...]
    y_ret = _silu(z_cols(o_gret, d_ret)) * o_n

    y_lru = [ycat_ref[i] * jax.nn.gelu(z_cols(o_glru + i * LANES, LANES))
             for i in range(lru_slabs)]
    y_ssm_t = [ycat_ref[lru_slabs + i] for i in range(ssm_slabs)]
    ycat = jnp.concatenate([v.astype(BF16) for v in y_lru] + [y_ret.astype(BF16)]
                           + [v.astype(BF16) for v in y_ssm_t], axis=1)
    m = jnp.dot(ycat, w_out_ref[...], preferred_element_type=F32)
    y = alpha * x + gate * m
    n_ln_blocks = max(1, len(proj_chunks))
    ln_rows = tile // n_ln_blocks
    for r in range(n_ln_blocks):
        rows = slice(r * ln_rows, (r + 1) * ln_rows)
        o_ref[0, rows, :] = _layer_norm_rows(y[rows], lng_ref[...], lnb_ref[...])
        project(1)


def _mixer_call(x, pos3, mod, prm, ln_g, ln_b, *, layer, alpha, tile):
    bsz, seq, d = x.shape
    depth = prm["lam"].shape[0]
    d_lru = prm["lam"].shape[2]
    d_ret = prm["gn_g"].shape[2]
    d_ssm = prm["s5_d"].shape[2]
    n_state = prm["s5_lr"].shape[2]
    n_in = prm["w_in"].shape[2]
    n_scan_slabs = (d_lru + d_ssm) // LANES
    tiles_per_seq = seq // tile
    n_tiles = bsz * tiles_per_seq
    mod_rows = mod.shape[0] // (depth * 3)
    names = ["w_in", "conv_w", "conv_b", "wg", "bg", "lam", "lg", "gn_g", "gn_b", "avg",
             "s5_lr", "s5_li", "s5_dt", "s5_bre", "s5_bim", "s5_c", "s5_d", "wglu", "bglu", "w_out"]
    consts = [prm[k] for k in names]
    scratch = [
        pltpu.VMEM((tile, n_in), F32),
        pltpu.VMEM((n_scan_slabs, tile, LANES), F32),
        pltpu.VMEM((tile, n_in), F32),
        pltpu.VMEM((n_scan_slabs, tile, LANES), F32),
        pltpu.VMEM((n_scan_slabs, tile, LANES), F32),
        pltpu.VMEM(((CONV_WIDTH - 1) * SUBLANES, d_lru), F32),
        pltpu.VMEM((1, d_lru), F32),
        pltpu.VMEM((tile, 2 * n_state), F32),
        pltpu.VMEM((tile, 2 * n_state), BF16),
        pltpu.VMEM((2, n_state), F32),
        pltpu.VMEM((_ZT_ROWS, SUBLANES, n_state), F32),
        pltpu.VMEM((d_ssm, 2 * n_state), BF16),
        pltpu.VMEM((d_ret // LANES, RET_CHUNK, 2 * RET_CHUNK), F32),
        pltpu.VMEM((RET_CHUNK, d_ret), F32),
        pltpu.VMEM((RET_CHUNK, d_ret), F32),
        pltpu.VMEM((1, d_ret), F32),
        pltpu.VMEM((d_ret // LANES, LANES, LANES), F32),
    ]

    def proj_tile(g):
        return jnp.minimum(g, n_tiles - 1)

    def mix_tile(g):
        return jnp.maximum(g - 1, 0)

    def x_index(t):
        return (t // tiles_per_seq, t % tiles_per_seq, 0)

    def mod_index(t):
        return ((layer * mod_rows + t // tiles_per_seq) * 3 + 1, 0, 0)

    return pl.pallas_call(
        functools.partial(_mixer_kernel, alpha=alpha, tile=tile, tiles_per_seq=tiles_per_seq,
                          d_lru=d_lru, d_ret=d_ret, d_ssm=d_ssm),
        grid=(n_tiles + 1,),
        in_specs=[
            pl.BlockSpec((1, tile, d), lambda g: x_index(proj_tile(g))),
            pl.BlockSpec((None, 3, d), lambda g: mod_index(proj_tile(g))),
            pl.BlockSpec((1, tile, d), lambda g: x_index(mix_tile(g))),
            pl.BlockSpec((1, 1, tile), lambda g: (mix_tile(g), 0, 0)),
            pl.BlockSpec((None, 3, d), lambda g: mod_index(mix_tile(g))),
        ] + [_layer_spec(a.shape, layer) for a in consts]
          + [_layer_spec(ln_g.shape, layer * 3 + 1), _layer_spec(ln_b.shape, layer * 3 + 1)],
        out_specs=pl.BlockSpec((1, tile, d), lambda g: x_index(mix_tile(g))),
        out_shape=jax.ShapeDtypeStruct(x.shape, F32),
        scratch_shapes=scratch,
        compiler_params=pltpu.CompilerParams(
            dimension_semantics=("arbitrary",),
            vmem_limit_bytes=VMEM_LIMIT_BYTES),
    )(x, mod, x, pos3, mod, *consts, ln_g, ln_b)


def _block_diag(blocks):
    depth, n, r, c = blocks.shape
    eye = jnp.eye(n, dtype=blocks.dtype)
    return (eye[None, :, None, :, None] * blocks[:, :, :, None, :]).reshape(depth, n * r, n * c)


def _gate_groups(a, b):
    n = a.shape[-1]
    parts = []
    for c0 in range(0, n, MXU_DIM):
        parts += [a[..., c0:c0 + MXU_DIM], b[..., c0:c0 + MXU_DIM]]
    return jnp.concatenate(parts, axis=-1)


def _mixer_params(mix_w_in, conv_w, conv_b, lru_wa, lru_ba, lru_wx, lru_bx, lru_lam,
                  ret_gn_g, ret_gn_b, ssm_lam_re, ssm_lam_im, ssm_log_step, ssm_b_re, ssm_b_im,
                  ssm_c_re, ssm_c_im, ssm_d, ssm_w_glu, ssm_b_glu, mix_w_out):
    depth, d_lru = lru_lam.shape
    d_ret = ret_gn_g.shape[1]
    d_ssm = ssm_d.shape[1]
    groups, n_per = ssm_lam_re.shape[1], ssm_lam_re.shape[2]
    n_state = groups * n_per
    head_dim = d_ret // RET_HEADS
    log_gamma = np.array([math.log1p(-2.0 ** (-5.0 - hh)) for hh in range(RET_HEADS)], np.float64)
    lg_lane = np.repeat(log_gamma, head_dim)[None, None, :].repeat(depth, axis=0)
    avg = np.kron(np.eye(RET_HEADS), np.full((head_dim, head_dim), 1.0 / head_dim))[None].repeat(depth, axis=0)
    c_re_bd = _block_diag(jnp.swapaxes(ssm_c_re, 2, 3))
    c_im_bd = _block_diag(jnp.swapaxes(ssm_c_im, 2, 3))
    return {
        "w_in": mix_w_in.astype(BF16),
        "conv_w": conv_w,
        "conv_b": conv_b.reshape(depth, 1, d_lru),
        "wg": _gate_groups(_block_diag(lru_wa), _block_diag(lru_wx)).astype(BF16),
        "bg": _gate_groups(lru_ba[:, None, :], lru_bx[:, None, :]),
        "lam": lru_lam.reshape(depth, 1, d_lru),
        "lg": jnp.asarray(lg_lane, F32),
        "gn_g": ret_gn_g.reshape(depth, 1, d_ret),
        "gn_b": ret_gn_b.reshape(depth, 1, d_ret),
        "avg": jnp.asarray(avg, BF16),
        "s5_lr": ssm_lam_re.reshape(depth, 1, n_state),
        "s5_li": ssm_lam_im.reshape(depth, 1, n_state),
        "s5_dt": jnp.repeat(ssm_log_step, n_per, axis=1).reshape(depth, 1, n_state),
        "s5_bre": _block_diag(jnp.swapaxes(ssm_b_re, 2, 3)),
        "s5_bim": _block_diag(jnp.swapaxes(ssm_b_im, 2, 3)),
        "s5_c": jnp.concatenate([c_re_bd, -c_im_bd], axis=1).astype(BF16),
        "s5_d": ssm_d.reshape(depth, 1, d_ssm),
        "wglu": ssm_w_glu.astype(BF16),
        "bglu": ssm_b_glu.reshape(depth, 1, d_ssm),
        "w_out": mix_w_out.astype(BF16),
    }


def _tiles(seq):
    tm = 512 if seq % 512 == 0 else seq
    tile = 512 if seq % 512 == 0 else seq
    return tm, tile


def kernel(x, c, positions, ada_w, ada_b, ln_g, ln_b, ffn1_w1, ffn1_w3, ffn1_w2, mix_w_in, conv_w, conv_b, lru_wa, lru_ba, lru_wx, lru_bx, lru_lam, ret_gn_g, ret_gn_b, ssm_lam_re, ssm_lam_im, ssm_log_step, ssm_b_re, ssm_b_im, ssm_c_re, ssm_c_im, ssm_d, ssm_w_glu, ssm_b_glu, mix_w_out, ffn2_w1, ffn2_w3, ffn2_w2):
    bsz, seq, d = x.shape
    depth = ada_w.shape[0]
    alpha = (2.0 * depth) ** 0.25
    tm, tile = _tiles(seq)
    f_chunk = 256

    rows = -(-bsz // SUBLANES) * SUBLANES
    c_pad = jnp.pad(c, ((0, rows - bsz), (0, 0)))
    mod = _ada_call(c_pad, ada_w, ada_b).reshape(depth * rows * 3, 3, d)
    pos3 = positions.reshape(bsz * (seq // tile), 1, tile)
    lng = ln_g.reshape(depth * 3, 1, d)
    lnb = ln_b.reshape(depth * 3, 1, d)
    ffn1 = (ffn1_w1.astype(BF16), ffn1_w3.astype(BF16), ffn1_w2.astype(BF16))
    ffn2 = (ffn2_w1.astype(BF16), ffn2_w3.astype(BF16), ffn2_w2.astype(BF16))
    prm = _mixer_params(mix_w_in, conv_w, conv_b, lru_wa, lru_ba, lru_wx, lru_bx, lru_lam,
                        ret_gn_g, ret_gn_b, ssm_lam_re, ssm_lam_im, ssm_log_step, ssm_b_re, ssm_b_im,
                        ssm_c_re, ssm_c_im, ssm_d, ssm_w_glu, ssm_b_glu, mix_w_out)

    for l in range(depth):
        x = _ffn_call(x, mod, *ffn1, lng, lnb, layer=l, sub=0, alpha=alpha, tm=tm, f_chunk=f_chunk)
        x = _mixer_call(x, pos3, mod, prm, lng, lnb, layer=l, alpha=alpha, tile=tile)
        x = _ffn_call(x, mod, *ffn2, lng, lnb, layer=l, sub=2, alpha=alpha, tm=tm, f_chunk=f_chunk)
    return x
```

```python
import functools
import math

import numpy as np
import jax
import jax.numpy as jnp
from jax import lax
from jax.experimental import pallas as pl
from jax.experimental.pallas import tpu as pltpu

F32 = jnp.float32
BF16 = jnp.bfloat16

LRU_HEADS = 6
CONV_WIDTH = 4
LRU_C = 8.0
RET_HEADS = 6
RET_CHUNK = 128
ROPE_BASE = 10000.0
SSM_GROUP = 16
SSM_STATE = 64
N_MOD = 9
MACARON_HALF = 0.5
LN_EPS = 1e-5

LANES = 128
SUBLANES = 8
VMEM_LIMIT_BYTES = 56 * 1024 * 1024
SCAN_BLOCK = SUBLANES * SUBLANES
MXU_DIM = 256
PROJ_CHUNK = MXU_DIM


def _silu(v):
    return v * jax.nn.sigmoid(v)


def _layer_norm_rows(v, g, b):
    mu = jnp.mean(v, axis=-1, keepdims=True)
    d = v - mu
    var = jnp.mean(d * d, axis=-1, keepdims=True)
    return d * lax.rsqrt(var + LN_EPS) * g + b


def _layer_spec(shape, layer):
    tail = (0,) * (len(shape) - 1)
    return pl.BlockSpec((None,) + tuple(shape[1:]), lambda *_: (layer,) + tail,
                        pipeline_mode=pl.Buffered(1))


def _ada_kernel(c_ref, w_ref, b_ref, o_ref):
    cond = _silu(c_ref[...]).astype(BF16)
    o_ref[0] = jnp.dot(cond, w_ref[0].astype(BF16), preferred_element_type=F32) + b_ref[0]


def _ada_call(c_pad, ada_w, ada_b):
    depth, d, n = ada_w.shape
    rows = c_pad.shape[0]
    tn = n // N_MOD
    return pl.pallas_call(
        _ada_kernel,
        grid=(depth, n // tn),
        in_specs=[
            pl.BlockSpec((rows, d), lambda l, j: (0, 0)),
            pl.BlockSpec((1, d, tn), lambda l, j: (l, 0, j)),
            pl.BlockSpec((1, 1, tn), lambda l, j: (l, 0, j)),
        ],
        out_specs=pl.BlockSpec((1, rows, tn), lambda l, j: (l, 0, j)),
        out_shape=jax.ShapeDtypeStruct((depth, rows, n), F32),
        compiler_params=pltpu.CompilerParams(
            dimension_semantics=("arbitrary", "arbitrary"),
            vmem_limit_bytes=VMEM_LIMIT_BYTES),
    )(c_pad, ada_w, ada_b.reshape(depth, 1, n))


def _ffn_kernel(x_ref, mod_ref, w1_ref, w3_ref, w2_ref, lng_ref, lnb_ref, o_ref, y_ref,
                *, alpha, f_chunk, n_tiles):
    tm = x_ref.shape[1]
    d_ff = w1_ref.shape[1]
    f_starts = list(range(0, d_ff, f_chunk))
    ln_rows = tm // SUBLANES

    def step(do_matmul, do_norm):
        ln_blocks = list(range(0, tm, ln_rows)) if do_norm else []
        anchor = {"from_mm": None, "from_ln": None}

        def norm_rows():
            if ln_blocks:
                r0 = ln_blocks.pop(0)
                y_blk = y_ref[r0:r0 + ln_rows, :]
                if anchor["from_mm"] is not None:
                    y_blk = y_blk + anchor["from_mm"]
                out = _layer_norm_rows(y_blk, lng_ref[...], lnb_ref[...])
                o_ref[0, r0:r0 + ln_rows, :] = out
                anchor["from_ln"] = out[0:1, :] * 0.0

        if do_matmul:
            x = x_ref[0]
            shift = mod_ref[0:1, :]
            scale = mod_ref[1:2, :]
            gate = mod_ref[2:3, :]
            h = (x * (1.0 + scale) + shift).astype(BF16)
            acc = jnp.zeros(x.shape, F32)
            for f0 in f_starts:
                h_c = h
                if anchor["from_ln"] is not None:
                    top = h[0:2 * SUBLANES, :] + anchor["from_ln"].astype(BF16)
                    h_c = jnp.concatenate([top, h[2 * SUBLANES:, :]], axis=0)
                a = jnp.dot(h_c, w1_ref[:, f0:f0 + f_chunk], preferred_element_type=F32)
                b = jnp.dot(h_c, w3_ref[:, f0:f0 + f_chunk], preferred_element_type=F32)
                g = (_silu(a) * b).astype(BF16)
                norm_rows()
                acc = acc + jnp.dot(g, w2_ref[f0:f0 + f_chunk, :], preferred_element_type=F32)
                if do_norm:
                    anchor["from_mm"] = a[0:1, 0:1] * 0.0
        while ln_blocks:
            norm_rows()
        if do_matmul:
            y_ref[...] = alpha * x + (MACARON_HALF * gate) * acc

    g_idx = pl.program_id(0)
    pl.when(g_idx == 0)(functools.partial(step, True, False))
    pl.when(jnp.logical_and(g_idx > 0, g_idx < n_tiles))(functools.partial(step, True, True))
    pl.when(g_idx == n_tiles)(functools.partial(step, False, True))


def _ffn_call(x, mod, w1, w3, w2, ln_g, ln_b, *, layer, sub, alpha, tm, f_chunk):
    bsz, seq, d = x.shape
    mod_rows = mod.shape[0] // (w1.shape[0] * 3)
    tiles_per_seq = seq // tm
    n_tiles = bsz * tiles_per_seq

    def mm_tile(g):
        return jnp.minimum(g, n_tiles - 1)

    def ln_tile(g):
        return jnp.maximum(g - 1, 0)

    def x_index(t):
        return (t // tiles_per_seq, t % tiles_per_seq, 0)

    return pl.pallas_call(
        functools.partial(_ffn_kernel, alpha=alpha, f_chunk=f_chunk, n_tiles=n_tiles),
        grid=(n_tiles + 1,),
        in_specs=[
            pl.BlockSpec((1, tm, d), lambda g: x_index(mm_tile(g))),
            pl.BlockSpec((None, 3, d), lambda g: ((layer * mod_rows + mm_tile(g) // tiles_per_seq) * 3 + sub, 0, 0)),
            _layer_spec(w1.shape, layer),
            _layer_spec(w3.shape, layer),
            _layer_spec(w2.shape, layer),
            _layer_spec(ln_g.shape, layer * 3 + sub),
            _layer_spec(ln_b.shape, layer * 3 + sub),
        ],
        out_specs=pl.BlockSpec((1, tm, d), lambda g: x_index(ln_tile(g))),
        out_shape=jax.ShapeDtypeStruct(x.shape, F32),
        scratch_shapes=[pltpu.VMEM((tm, d), F32)],
        compiler_params=pltpu.CompilerParams(
            dimension_semantics=("arbitrary",),
            vmem_limit_bytes=VMEM_LIMIT_BYTES),
    )(x, mod, w1, w3, w2, ln_g, ln_b)


_ZT_STEP_RE = 0
_ZT_STEP_IM = SUBLANES
_ZT_HS_RE = 2 * SUBLANES
_ZT_HS_IM = 2 * SUBLANES + 3
_ZT_ROWS = 2 * SUBLANES + 6
_HS_SHIFTS = (1, 2, 4)


def _swap_halves(v, lane_in_head, half):
    n = v.shape[-1]
    fwd = pltpu.roll(v, half, 1)
    bwd = pltpu.roll(v, n - half, 1)
    return jnp.where(lane_in_head < half, bwd, fwd)


def _row_bcast(v, row):
    return jnp.broadcast_to(v[row:row + 1, :], v.shape)


_N_MIXER_IO = 28


def _mixer_kernel(*refs, **static):
    io, (z_a, zs_a, z_b, zs_b), rest = refs[:_N_MIXER_IO], refs[_N_MIXER_IO:_N_MIXER_IO + 4], refs[_N_MIXER_IO + 4:]
    n_tiles = static.pop("n_tiles")
    g_idx = pl.program_id(0)
    inner = jnp.logical_and(g_idx > 0, g_idx < n_tiles)
    even = (z_a, zs_a, z_b, zs_b)
    odd = (z_b, zs_b, z_a, zs_a)

    def run(bufs, **phases):
        return functools.partial(_mixer_step, *io, *bufs, *rest, **static, **phases)

    pl.when(g_idx == 0)(run(even, do_proj=True, do_mix=False))
    pl.when(jnp.logical_and(inner, g_idx % 2 == 0))(run(even, do_proj=True, do_mix=True))
    pl.when(jnp.logical_and(inner, g_idx % 2 == 1))(run(odd, do_proj=True, do_mix=True))
    pl.when(g_idx == n_tiles)(run(even if n_tiles % 2 == 0 else odd, do_proj=False, do_mix=True))


def _mixer_step(
        xp_ref, modp_ref,
        x_ref, pos_ref, mod_ref, w_in_ref, conv_w_ref, conv_b_ref, wg_ref, bg_ref, lam_ref,
        lg_ref, gn_g_ref, gn_b_ref, avg_ref,
        s5_lr_ref, s5_li_ref, s5_dt_ref, s5_bre_ref, s5_bim_ref, s5_c_ref, s5_d_ref,
        wglu_ref, bglu_ref, w_out_ref, lng_ref, lnb_ref,
        o_ref,
        zp_ref, zsp_ref, zc_ref, zsc_ref,
        ycat_ref, tail_ref, hcar_ref, bu_ref, xb_ref, xcar_ref, ztab_ref, bbar_ref,
        dmask_ref, qdec_ref, kdec_ref, cdec_ref, rstate_ref,
        *, alpha, tile, tiles_per_seq, d_lru, d_ret, d_ssm, do_proj, do_mix):
    g_idx = pl.program_id(0)
    n_state = bu_ref.shape[1] // 2
    n_blk = tile // SCAN_BLOCK
    lru_slabs = d_lru // LANES
    ssm_slabs = d_ssm // LANES
    ret_pairs = d_ret // LANES
    head_dim = d_ret // RET_HEADS
    half = head_dim // 2

    def _init_tables():
        lr = s5_lr_ref[...]
        li = s5_li_ref[...]
        dt = jnp.exp(s5_dt_ref[...])
        sub = lax.broadcasted_iota(jnp.int32, (SUBLANES, n_state), 0)

        def z_power(n):
            mag = jnp.exp(n * (lr * dt))
            ang = n * (li * dt)
            return mag * jnp.cos(ang), mag * jnp.sin(ang)

        st_r, st_i = z_power((sub + 1).astype(F32))
        ch_r, ch_i = z_power((SUBLANES * (sub + 1)).astype(F32))
        for b in range(SUBLANES):
            ztab_ref[_ZT_STEP_RE + b] = _row_bcast(st_r, b)
            ztab_ref[_ZT_STEP_IM + b] = _row_bcast(st_i, b)
        for n, d in enumerate(_HS_SHIFTS):
            ztab_ref[_ZT_HS_RE + n] = jnp.where(sub >= d, _row_bcast(ch_r, d - 1), 0.0)
            ztab_ref[_ZT_HS_IM + n] = jnp.where(sub >= d, _row_bcast(ch_i, d - 1), 0.0)
        zr = st_r[0:1, :]
        zi = st_i[0:1, :]
        den = lr * lr + li * li
        er = ((zr - 1.0) * lr + zi * li) / den
        ei = (zi * lr - (zr - 1.0) * li) / den
        bre = s5_bre_ref[...]
        bim = s5_bim_ref[...]
        bbar_ref[:, 0:n_state] = (er * bre - ei * bim).astype(BF16)
        bbar_ref[:, n_state:2 * n_state] = (er * bim + ei * bre).astype(BF16)
        tq = lax.broadcasted_iota(jnp.int32, (RET_CHUNK, 2 * RET_CHUNK), 0)
        tk = lax.broadcasted_iota(jnp.int32, (RET_CHUNK, 2 * RET_CHUNK), 1) % RET_CHUNK
        diff = (tq - tk).astype(F32)
        row = lax.broadcasted_iota(jnp.int32, (RET_CHUNK, d_ret), 0).astype(F32)
        lg_row = lg_ref[...]
        qdec_ref[...] = jnp.exp(lg_row * (row + 1.0))
        kdec_ref[...] = jnp.exp(lg_row * (RET_CHUNK - 1.0 - row))
        cdec_ref[...] = jnp.exp(lg_row * float(RET_CHUNK))
        for p in range(ret_pairs):
            lg_lo = lg_ref[0:1, p * LANES:p * LANES + 1]
            lg_hi = lg_ref[0:1, p * LANES + head_dim:p * LANES + head_dim + 1]
            col = lax.broadcasted_iota(jnp.int32, (RET_CHUNK, 2 * RET_CHUNK), 1)
            lg_pair = jnp.where(col < RET_CHUNK, lg_lo, lg_hi)
            dmask_ref[p] = jnp.where(diff >= 0.0, jnp.exp(lg_pair * jnp.maximum(diff, 0.0)), 0.0)

    if not do_mix:
        _init_tables()

    if do_mix:
        @pl.when((g_idx - 1) % tiles_per_seq == 0)
        def _reset_state():
            tail_ref[...] = jnp.zeros(tail_ref.shape, F32)
            hcar_ref[...] = jnp.zeros(hcar_ref.shape, F32)
            xcar_ref[...] = jnp.zeros(xcar_ref.shape, F32)
            rstate_ref[...] = jnp.zeros(rstate_ref.shape, F32)

    o_lru, o_glru = 0, d_lru
    o_q, o_k, o_v, o_gret = 2 * d_lru, 2 * d_lru + d_ret, 2 * d_lru + 2 * d_ret, 2 * d_lru + 3 * d_ret
    o_ssm = 2 * d_lru + 4 * d_ret

    if do_proj:
        xp = xp_ref[0]
        hp = (xp * (1.0 + modp_ref[1:2, :]) + modp_ref[0:1, :]).astype(BF16)
    scan_cols = ([o_lru + i * LANES for i in range(lru_slabs)]
                 + [o_ssm + i * LANES for i in range(ssm_slabs)])
    proj_chunks = list(range(0, w_in_ref.shape[1], PROJ_CHUNK)) if do_proj else []

    def project(n_chunks):
        for _ in range(min(n_chunks, len(proj_chunks))):
            c0 = proj_chunks.pop(0)
            zchunk = jnp.dot(hp, w_in_ref[:, c0:c0 + PROJ_CHUNK], preferred_element_type=F32)
            zp_ref[:, c0:c0 + PROJ_CHUNK] = zchunk
            for s_idx, sc in enumerate(scan_cols):
                if c0 <= sc < c0 + PROJ_CHUNK:
                    zsp_ref[s_idx] = zchunk[:, sc - c0:sc - c0 + LANES]

    if not do_mix:
        project(len(proj_chunks))
        return

    x = x_ref[0]
    gate = mod_ref[2:3, :]

    def z_cols(c0, width, r0=0, n_rows=tile):
        return zc_ref[r0:r0 + n_rows, c0:c0 + width]

    def scan_rows(k, b):
        return pl.ds(k * SCAN_BLOCK + b, SUBLANES, stride=SUBLANES)

    def blk_rows(k, b):
        return slice(k * SCAN_BLOCK + b * SUBLANES, k * SCAN_BLOCK + (b + 1) * SUBLANES)

    u_rows = [[jnp.concatenate([zsc_ref[i, scan_rows(k, b), :] for i in range(lru_slabs)], axis=1)
               for b in range(SUBLANES)] for k in range(n_blk)]
    us_rows = [[jnp.concatenate([zsc_ref[lru_slabs + i, scan_rows(k, b), :] for i in range(ssm_slabs)], axis=1)
                for b in range(SUBLANES)] for k in range(n_blk)]

    sub_l = lax.broadcasted_iota(jnp.int32, (SUBLANES, d_lru), 0)
    sub_s = lax.broadcasted_iota(jnp.int32, (SUBLANES, n_state), 0)

    pos_row = pos_ref[0].astype(F32)
    freq = lax.broadcasted_iota(jnp.int32, (half, 1), 0).astype(F32)
    inv = jnp.exp(freq * (-math.log(ROPE_BASE) / half))
    ang_t = inv * pos_row
    cos_t = jnp.cos(ang_t)
    sin_t = jnp.sin(ang_t)
    reps = LANES // head_dim
    cos_tab = jnp.concatenate([cos_t, cos_t] * reps, axis=0).T
    sin_tab = jnp.concatenate([-sin_t, sin_t] * reps, axis=0).T
    lane_in_head = lax.broadcasted_iota(jnp.int32, (RET_CHUNK, LANES), 1) % head_dim
    lane = lax.broadcasted_iota(jnp.int32, (RET_CHUNK, LANES), 1)
    lo_mask = lane < head_dim
    blk_r = lax.broadcasted_iota(jnp.int32, (LANES, LANES), 0) // head_dim
    blk_c = lax.broadcasted_iota(jnp.int32, (LANES, LANES), 1) // head_dim
    same_head = blk_r == blk_c
    inv_sqrt_dh = head_dim ** -0.5
    ret_state = [rstate_ref[p] for p in range(ret_pairs)]
    ret_out = [[] for _ in range(ret_pairs)]
    ret_units = [(ci_, p) for ci_ in range(tile // RET_CHUNK) for p in range(ret_pairs)]

    def retention_units(n_units):
        for _ in range(min(n_units, len(ret_units))):
            ci_, p = ret_units.pop(0)
            rows = slice(ci_ * RET_CHUNK, (ci_ + 1) * RET_CHUNK)
            cols = slice(p * LANES, (p + 1) * LANES)
            cs = cos_tab[rows]
            sn = sin_tab[rows]
            q_c = z_cols(o_q + p * LANES, LANES, ci_ * RET_CHUNK, RET_CHUNK)
            k_c = z_cols(o_k + p * LANES, LANES, ci_ * RET_CHUNK, RET_CHUNK)
            v_c = z_cols(o_v + p * LANES, LANES, ci_ * RET_CHUNK, RET_CHUNK)
            qr = q_c * cs + _swap_halves(q_c, lane_in_head, half) * sn
            kr = (k_c * cs + _swap_halves(k_c, lane_in_head, half) * sn) * inv_sqrt_dh
            k_bd = jnp.concatenate([jnp.where(lo_mask, kr, 0.0), jnp.where(lo_mask, 0.0, kr)], axis=0)
            v_bd = jnp.concatenate([jnp.where(lo_mask, v_c, 0.0), jnp.where(lo_mask, 0.0, v_c)], axis=0)
            scores = lax.dot_general(qr.astype(BF16), k_bd.astype(BF16), (((1,), (1,)), ((), ())),
                                     preferred_element_type=F32)
            scores = scores * dmask_ref[p]
            o_c = jnp.dot(scores.astype(BF16), v_bd.astype(BF16), preferred_element_type=F32)
            o_c = o_c + jnp.dot((qr * qdec_ref[:, cols]).astype(BF16), ret_state[p].astype(BF16),
                                preferred_element_type=F32)
            kv = jnp.dot((kr * kdec_ref[:, cols]).T.astype(BF16), v_c.astype(BF16), preferred_element_type=F32)
            ret_state[p] = cdec_ref[:, cols] * ret_state[p] + jnp.where(same_head, kv, 0.0)
            ret_out[p].append(o_c)

    n_tail = CONV_WIDTH - 1
    uc_rows = []
    for k in range(n_blk):
        early = {}
        for dd in range(1, n_tail + 1):
            cur = u_rows[k][SUBLANES - dd]
            if k == 0:
                prev = tail_ref[(n_tail - dd) * SUBLANES:(n_tail - dd + 1) * SUBLANES, :]
            else:
                prev = u_rows[k - 1][SUBLANES - dd]
            early[-dd] = pltpu.roll(jnp.where(sub_l == SUBLANES - 1, prev, cur), 1, 0)
        for b in range(SUBLANES):
            acc = conv_b_ref[...]
            for kk in range(CONV_WIDTH):
                src = b - n_tail + kk
                acc = acc + (u_rows[k][src] if src >= 0 else early[src]) * conv_w_ref[kk:kk + 1, :]
            uc_rows.append(acc)
    for dd in range(1, n_tail + 1):
        tail_ref[(n_tail - dd) * SUBLANES:(n_tail - dd + 1) * SUBLANES, :] = u_rows[n_blk - 1][SUBLANES - dd]
    project(1)
    uc = jnp.concatenate(uc_rows, axis=0)
    uc_bf = uc.astype(BF16)
    r_parts, i_parts = [], []
    for c0 in range(0, d_lru, MXU_DIM):
        wid = min(MXU_DIM, d_lru - c0)
        gz = (jnp.dot(uc_bf[:, c0:c0 + wid], wg_ref[c0:c0 + wid, 2 * c0:2 * (c0 + wid)],
                      preferred_element_type=F32) + bg_ref[:, 2 * c0:2 * (c0 + wid)])
        r_parts.append(gz[:, 0:wid])
        i_parts.append(gz[:, wid:2 * wid])
    r_gate = jax.nn.sigmoid(jnp.concatenate(r_parts, axis=1))
    i_gate = jax.nn.sigmoid(jnp.concatenate(i_parts, axis=1))
    project(1)
    lam = lam_ref[...]
    softplus_neg_lam = jnp.maximum(-lam, 0.0) + jnp.log(1.0 + jnp.exp(-jnp.abs(lam)))
    a_coef = jnp.exp((-LRU_C * softplus_neg_lam) * r_gate)
    b_term = jnp.sqrt(1.0 - a_coef * a_coef) * (i_gate * uc)
    project(1)

    h_car = jnp.broadcast_to(hcar_ref[...], (SUBLANES, d_lru))
    for k in range(n_blk):
        if k % 4 == 0:
            project(1)
        h_prev = jnp.where(sub_l == 0, h_car, 0.0)
        h_loc, a_cum = [], []
        for b in range(SUBLANES):
            a_b = a_coef[blk_rows(k, b)]
            h_prev = a_b * h_prev + b_term[blk_rows(k, b)]
            h_loc.append(h_prev)
            a_cum.append(a_b if b == 0 else a_b * a_cum[-1])
        pa, pb = a_cum[-1], h_loc[-1]
        for d in _HS_SHIFTS:
            live = sub_l >= d
            pb = jnp.where(live, pa * pltpu.roll(pb, d, 0) + pb, pb)
            pa = jnp.where(live, pa * pltpu.roll(pa, d, 0), pa)
        h_in = jnp.where(sub_l == 0, 0.0, pltpu.roll(pb, 1, 0))
        h_car = _row_bcast(pb, SUBLANES - 1)
        for b in range(SUBLANES):
            h_t = h_loc[b] + a_cum[b] * h_in
            for i in range(lru_slabs):
                ycat_ref[i, scan_rows(k, b), :] = h_t[:, i * LANES:(i + 1) * LANES]
    hcar_ref[...] = h_car[0:1, :]

    us_p = jnp.concatenate([us_rows[k][b] for k in range(n_blk) for b in range(SUBLANES)], axis=0)
    bu_ref[...] = jnp.dot(us_p.astype(BF16), bbar_ref[...], preferred_element_type=F32)
    re_cols = slice(0, n_state)
    im_cols = slice(n_state, 2 * n_state)
    zr = ztab_ref[_ZT_STEP_RE]
    zi = ztab_ref[_ZT_STEP_IM]
    xc_r = jnp.broadcast_to(xcar_ref[0:1, :], (SUBLANES, n_state))
    xc_i = jnp.broadcast_to(xcar_ref[1:2, :], (SUBLANES, n_state))
    for k in range(n_blk):
        if k % 3 == 0:
            project(1)
        retention_units(2 - k % 2)
        xr = jnp.where(sub_s == 0, xc_r, 0.0)
        xi = jnp.where(sub_s == 0, xc_i, 0.0)
        for b in range(SUBLANES):
            xr, xi = (zr * xr - zi * xi + bu_ref[blk_rows(k, b), re_cols],
                      zr * xi + zi * xr + bu_ref[blk_rows(k, b), im_cols])
            bu_ref[blk_rows(k, b), re_cols] = xr
            bu_ref[blk_rows(k, b), im_cols] = xi
        pr, pi = xr, xi
        for n, d in enumerate(_HS_SHIFTS):
            hr, hi = ztab_ref[_ZT_HS_RE + n], ztab_ref[_ZT_HS_IM + n]
            sr, si = pltpu.roll(pr, d, 0), pltpu.roll(pi, d, 0)
            pr, pi = pr + hr * sr - hi * si, pi + hr * si + hi * sr
        xin_r = jnp.where(sub_s == 0, 0.0, pltpu.roll(pr, 1, 0))
        xin_i = jnp.where(sub_s == 0, 0.0, pltpu.roll(pi, 1, 0))
        xc_r = _row_bcast(pr, SUBLANES - 1)
        xc_i = _row_bcast(pi, SUBLANES - 1)
        for b0 in range(0, SUBLANES, 2):
            halves_r, halves_i = [], []
            for b in (b0, b0 + 1):
                sr, si = ztab_ref[_ZT_STEP_RE + b], ztab_ref[_ZT_STEP_IM + b]
                halves_r.append(bu_ref[blk_rows(k, b), re_cols] + sr * xin_r - si * xin_i)
                halves_i.append(bu_ref[blk_rows(k, b), im_cols] + sr * xin_i + si * xin_r)
            rows2 = slice(k * SCAN_BLOCK + b0 * SUBLANES, k * SCAN_BLOCK + (b0 + 2) * SUBLANES)
            xb_ref[rows2, re_cols] = jnp.concatenate(halves_r, axis=0).astype(BF16)
            xb_ref[rows2, im_cols] = jnp.concatenate(halves_i, axis=0).astype(BF16)
    xcar_ref[0:1, :] = xc_r[0:1, :]
    xcar_ref[1:2, :] = xc_i[0:1, :]

    y_s = jnp.dot(xb_ref[...], s5_c_ref[...], preferred_element_type=F32)
    y_s = jax.nn.gelu(y_s + s5_d_ref[...] * us_p)
    glu = jnp.dot(y_s.astype(BF16), wglu_ref[...], preferred_element_type=F32) + bglu_ref[...]
    y_ssm = y_s * jax.nn.sigmoid(glu)
    for k in range(n_blk):
        for b in range(SUBLANES):
            for i in range(ssm_slabs):
                ycat_ref[lru_slabs + i, scan_rows(k, b), :] = y_ssm[blk_rows(k, b), i * LANES:(i + 1) * LANES]

    retention_units(len(ret_units))
    for p in range(ret_pairs):
        rstate_ref[p] = ret_state[p]
    o_all = jnp.concatenate([jnp.concatenate(ret_out[p], axis=0) for p in range(ret_pairs)], axis=1)
    def head_mean(v):
        parts = []
        for c0 in range(0, d_ret, MXU_DIM):
            c1 = min(c0 + MXU_DIM, d_ret)
            parts.append(jnp.dot(v[:, c0:c1].astype(BF16), avg_ref[c0:c1, c0:c1], preferred_element_type=F32))
        return jnp.concatenate(parts, axis=1)

    mu = head_mean(o_all)
    dev = o_all - mu
    d2 = dev * dev
    var = head_mean(d2)
    o_n = dev * lax.rsqrt(var + LN_EPS) * gn_g_ref[...] + gn_b_ref[...]
    y_ret = _silu(z_cols(o_gret, d_ret)) * o_n

    y_lru = [ycat_ref[i] * jax.nn.gelu(z_cols(o_glru + i * LANES, LANES))
             for i in range(lru_slabs)]
    y_ssm_t = [ycat_ref[lru_slabs + i] for i in range(ssm_slabs)]
    ycat = jnp.concatenate([v.astype(BF16) for v in y_lru] + [y_ret.astype(BF16)]
                           + [v.astype(BF16) for v in y_ssm_t], axis=1)
    m = jnp.dot(ycat, w_out_ref[...], preferred_element_type=F32)
    y = alpha * x + gate * m
    n_ln_blocks = max(1, len(proj_chunks))
    ln_rows = tile // n_ln_blocks
    for r in range(n_ln_blocks):
        rows = slice(r * ln_rows, (r + 1) * ln_rows)
        o_ref[0, rows, :] = _layer_norm_rows(y[rows], lng_ref[...], lnb_ref[...])
        project(1)


def _mixer_call(x, pos3, mod, prm, ln_g, ln_b, *, layer, alpha, tile):
    bsz, seq, d = x.shape
    depth = prm["lam"].shape[0]
    d_lru = prm["lam"].shape[2]
    d_ret = prm["gn_g"].shape[2]
    d_ssm = prm["s5_d"].shape[2]
    n_state = prm["s5_lr"].shape[2]
    n_in = prm["w_in"].shape[2]
    n_scan_slabs = (d_lru + d_ssm) // LANES
    tiles_per_seq = seq // tile
    n_tiles = bsz * tiles_per_seq
    mod_rows = mod.shape[0] // (depth * 3)
    names = ["w_in", "conv_w", "conv_b", "wg", "bg", "lam", "lg", "gn_g", "gn_b", "avg",
             "s5_lr", "s5_li", "s5_dt", "s5_bre", "s5_bim", "s5_c", "s5_d", "wglu", "bglu", "w_out"]
    consts = [prm[k] for k in names]
    scratch = [
        pltpu.VMEM((tile, n_in), F32),
        pltpu.VMEM((n_scan_slabs, tile, LANES), F32),
        pltpu.VMEM((tile, n_in), F32),
        pltpu.VMEM((n_scan_slabs, tile, LANES), F32),
        pltpu.VMEM((n_scan_slabs, tile, LANES), F32),
        pltpu.VMEM(((CONV_WIDTH - 1) * SUBLANES, d_lru), F32),
        pltpu.VMEM((1, d_lru), F32),
        pltpu.VMEM((tile, 2 * n_state), F32),
        pltpu.VMEM((tile, 2 * n_state), BF16),
        pltpu.VMEM((2, n_state), F32),
        pltpu.VMEM((_ZT_ROWS, SUBLANES, n_state), F32),
        pltpu.VMEM((d_ssm, 2 * n_state), BF16),
        pltpu.VMEM((d_ret // LANES, RET_CHUNK, 2 * RET_CHUNK), F32),
        pltpu.VMEM((RET_CHUNK, d_ret), F32),
        pltpu.VMEM((RET_CHUNK, d_ret), F32),
        pltpu.VMEM((1, d_ret), F32),
        pltpu.VMEM((d_ret // LANES, LANES, LANES), F32),
    ]

    def proj_tile(g):
        return jnp.minimum(g, n_tiles - 1)

    def mix_tile(g):
        return jnp.maximum(g - 1, 0)

    def x_index(t):
        return (t // tiles_per_seq, t % tiles_per_seq, 0)

    def mod_index(t):
        return ((layer * mod_rows + t // tiles_per_seq) * 3 + 1, 0, 0)

    return pl.pallas_call(
        functools.partial(_mixer_kernel, alpha=alpha, tile=tile, tiles_per_seq=tiles_per_seq,
                          n_tiles=n_tiles, d_lru=d_lru, d_ret=d_ret, d_ssm=d_ssm),
        grid=(n_tiles + 1,),
        in_specs=[
            pl.BlockSpec((1, tile, d), lambda g: x_index(proj_tile(g))),
            pl.BlockSpec((None, 3, d), lambda g: mod_index(proj_tile(g))),
            pl.BlockSpec((1, tile, d), lambda g: x_index(mix_tile(g))),
            pl.BlockSpec((1, 1, tile), lambda g: (mix_tile(g), 0, 0)),
            pl.BlockSpec((None, 3, d), lambda g: mod_index(mix_tile(g))),
        ] + [_layer_spec(a.shape, layer) for a in consts]
          + [_layer_spec(ln_g.shape, layer * 3 + 1), _layer_spec(ln_b.shape, layer * 3 + 1)],
        out_specs=pl.BlockSpec((1, tile, d), lambda g: x_index(mix_tile(g))),
        out_shape=jax.ShapeDtypeStruct(x.shape, F32),
        scratch_shapes=scratch,
        compiler_params=pltpu.CompilerParams(
            dimension_semantics=("arbitrary",),
            vmem_limit_bytes=VMEM_LIMIT_BYTES),
    )(x, mod, x, pos3, mod, *consts, ln_g, ln_b)


def _block_diag(blocks):
    depth, n, r, c = blocks.shape
    eye = jnp.eye(n, dtype=blocks.dtype)
    return (eye[None, :, None, :, None] * blocks[:, :, :, None, :]).reshape(depth, n * r, n * c)


def _gate_groups(a, b):
    n = a.shape[-1]
    parts = []
    for c0 in range(0, n, MXU_DIM):
        parts += [a[..., c0:c0 + MXU_DIM], b[..., c0:c0 + MXU_DIM]]
    return jnp.concatenate(parts, axis=-1)


def _mixer_params(mix_w_in, conv_w, conv_b, lru_wa, lru_ba, lru_wx, lru_bx, lru_lam,
                  ret_gn_g, ret_gn_b, ssm_lam_re, ssm_lam_im, ssm_log_step, ssm_b_re, ssm_b_im,
                  ssm_c_re, ssm_c_im, ssm_d, ssm_w_glu, ssm_b_glu, mix_w_out):
    depth, d_lru = lru_lam.shape
    d_ret = ret_gn_g.shape[1]
    d_ssm = ssm_d.shape[1]
    groups, n_per = ssm_lam_re.shape[1], ssm_lam_re.shape[2]
    n_state = groups * n_per
    head_dim = d_ret // RET_HEADS
    log_gamma = np.array([math.log1p(-2.0 ** (-5.0 - hh)) for hh in range(RET_HEADS)], np.float64)
    lg_lane = np.repeat(log_gamma, head_dim)[None, None, :].repeat(depth, axis=0)
    avg = np.kron(np.eye(RET_HEADS), np.full((head_dim, head_dim), 1.0 / head_dim))[None].repeat(depth, axis=0)
    c_re_bd = _block_diag(jnp.swapaxes(ssm_c_re, 2, 3))
    c_im_bd = _block_diag(jnp.swapaxes(ssm_c_im, 2, 3))
    return {
        "w_in": mix_w_in.astype(BF16),
        "conv_w": conv_w,
        "conv_b": conv_b.reshape(depth, 1, d_lru),
        "wg": _gate_groups(_block_diag(lru_wa), _block_diag(lru_wx)).astype(BF16),
        "bg": _gate_groups(lru_ba[:, None, :], lru_bx[:, None, :]),
        "lam": lru_lam.reshape(depth, 1, d_lru),
        "lg": jnp.asarray(lg_lane, F32),
        "gn_g": ret_gn_g.reshape(depth, 1, d_ret),
        "gn_b": ret_gn_b.reshape(depth, 1, d_ret),
        "avg": jnp.asarray(avg, BF16),
        "s5_lr": ssm_lam_re.reshape(depth, 1, n_state),
        "s5_li": ssm_lam_im.reshape(depth, 1, n_state),
        "s5_dt": jnp.repeat(ssm_log_step, n_per, axis=1).reshape(depth, 1, n_state),
        "s5_bre": _block_diag(jnp.swapaxes(ssm_b_re, 2, 3)),
        "s5_bim": _block_diag(jnp.swapaxes(ssm_b_im, 2, 3)),
        "s5_c": jnp.concatenate([c_re_bd, -c_im_bd], axis=1).astype(BF16),
        "s5_d": ssm_d.reshape(depth, 1, d_ssm),
        "wglu": ssm_w_glu.astype(BF16),
        "bglu": ssm_b_glu.reshape(depth, 1, d_ssm),
        "w_out": mix_w_out.astype(BF16),
    }


def _tiles(seq):
    tm = 1024 if seq % 1024 == 0 else seq
    tile = 512 if seq % 512 == 0 else seq
    return tm, tile


def kernel(x, c, positions, ada_w, ada_b, ln_g, ln_b, ffn1_w1, ffn1_w3, ffn1_w2, mix_w_in, conv_w, conv_b, lru_wa, lru_ba, lru_wx, lru_bx, lru_lam, ret_gn_g, ret_gn_b, ssm_lam_re, ssm_lam_im, ssm_log_step, ssm_b_re, ssm_b_im, ssm_c_re, ssm_c_im, ssm_d, ssm_w_glu, ssm_b_glu, mix_w_out, ffn2_w1, ffn2_w3, ffn2_w2):
    bsz, seq, d = x.shape
    depth = ada_w.shape[0]
    alpha = (2.0 * depth) ** 0.25
    tm, tile = _tiles(seq)
    f_chunk = 256

    rows = -(-bsz // SUBLANES) * SUBLANES
    c_pad = jnp.pad(c, ((0, rows - bsz), (0, 0)))
    mod = _ada_call(c_pad, ada_w, ada_b).reshape(depth * rows * 3, 3, d)
    pos3 = positions.reshape(bsz * (seq // tile), 1, tile)
    lng = ln_g.reshape(depth * 3, 1, d)
    lnb = ln_b.reshape(depth * 3, 1, d)
    ffn1 = (ffn1_w1.astype(BF16), ffn1_w3.astype(BF16), ffn1_w2.astype(BF16))
    ffn2 = (ffn2_w1.astype(BF16), ffn2_w3.astype(BF16), ffn2_w2.astype(BF16))
    prm = _mixer_params(mix_w_in, conv_w, conv_b, lru_wa, lru_ba, lru_wx, lru_bx, lru_lam,
                        ret_gn_g, ret_gn_b, ssm_lam_re, ssm_lam_im, ssm_log_step, ssm_b_re, ssm_b_im,
                        ssm_c_re, ssm_c_im, ssm_d, ssm_w_glu, ssm_b_glu, mix_w_out)

    for l in range(depth):
        x = _ffn_call(x, mod, *ffn1, lng, lnb, layer=l, sub=0, alpha=alpha, tm=tm, f_chunk=f_chunk)
        x = _mixer_call(x, pos3, mod, prm, lng, lnb, layer=l, alpha=alpha, tile=tile)
        x = _ffn_call(x, mod, *ffn2, lng, lnb, layer=l, sub=2, alpha=alpha, tm=tm, f_chunk=f_chunk)
    return x
```

```python
import functools
import math

import numpy as np
import jax
import jax.numpy as jnp
from jax import lax
from jax.experimental import pallas as pl
from jax.experimental.pallas import tpu as pltpu

F32 = jnp.float32
BF16 = jnp.bfloat16

LRU_HEADS = 6
CONV_WIDTH = 4
LRU_C = 8.0
RET_HEADS = 6
RET_CHUNK = 128
ROPE_BASE = 10000.0
SSM_GROUP = 16
SSM_STATE = 64
N_MOD = 9
MACARON_HALF = 0.5
LN_EPS = 1e-5

LANES = 128
SUBLANES = 8
VMEM_LIMIT_BYTES = 56 * 1024 * 1024
SCAN_BLOCK = SUBLANES * SUBLANES
MXU_DIM = 256
PROJ_CHUNK = MXU_DIM


def _silu(v):
    return v * jax.nn.sigmoid(v)


def _layer_norm_rows(v, g, b):
    mu = jnp.mean(v, axis=-1, keepdims=True)
    d = v - mu
    var = jnp.mean(d * d, axis=-1, keepdims=True)
    return d * lax.rsqrt(var + LN_EPS) * g + b


def _layer_spec(shape, layer):
    tail = (0,) * (len(shape) - 1)
    return pl.BlockSpec((None,) + tuple(shape[1:]), lambda *_: (layer,) + tail,
                        pipeline_mode=pl.Buffered(1))


def _ada_kernel(c_ref, w_ref, b_ref, o_ref):
    cond = _silu(c_ref[...]).astype(BF16)
    o_ref[0] = jnp.dot(cond, w_ref[0].astype(BF16), preferred_element_type=F32) + b_ref[0]


def _ada_call(c_pad, ada_w, ada_b):
    depth, d, n = ada_w.shape
    rows = c_pad.shape[0]
    tn = n // N_MOD
    return pl.pallas_call(
        _ada_kernel,
        grid=(depth, n // tn),
        in_specs=[
            pl.BlockSpec((rows, d), lambda l, j: (0, 0)),
            pl.BlockSpec((1, d, tn), lambda l, j: (l, 0, j)),
            pl.BlockSpec((1, 1, tn), lambda l, j: (l, 0, j)),
        ],
        out_specs=pl.BlockSpec((1, rows, tn), lambda l, j: (l, 0, j)),
        out_shape=jax.ShapeDtypeStruct((depth, rows, n), F32),
        compiler_params=pltpu.CompilerParams(
            dimension_semantics=("arbitrary", "arbitrary"),
            vmem_limit_bytes=VMEM_LIMIT_BYTES),
    )(c_pad, ada_w, ada_b.reshape(depth, 1, n))


def _ffn_kernel(x_ref, mod_ref, w1_ref, w3_ref, w2_ref, lng_ref, lnb_ref, o_ref, y_ref,
                *, alpha, f_chunk, n_tiles):
    tm = x_ref.shape[1]
    d_ff = w1_ref.shape[1]
    f_starts = list(range(0, d_ff, f_chunk))
    ln_rows = tm // SUBLANES

    def step(do_matmul, do_norm):
        ln_blocks = list(range(0, tm, ln_rows)) if do_norm else []
        anchor = {"from_mm": None, "from_ln": None}

        def norm_rows():
            if ln_blocks:
                r0 = ln_blocks.pop(0)
                y_blk = y_ref[r0:r0 + ln_rows, :]
                if anchor["from_mm"] is not None:
                    y_blk = y_blk + anchor["from_mm"]
                out = _layer_norm_rows(y_blk, lng_ref[...], lnb_ref[...])
                o_ref[0, r0:r0 + ln_rows, :] = out
                anchor["from_ln"] = out[0:1, :] * 0.0

        if do_matmul:
            x = x_ref[0]
            shift = mod_ref[0:1, :]
            scale = mod_ref[1:2, :]
            gate = mod_ref[2:3, :]
            h = (x * (1.0 + scale) + shift).astype(BF16)
            g_chunks = []
            for f0 in f_starts:
                h_c = h
                if anchor["from_ln"] is not None:
                    top = h[0:2 * SUBLANES, :] + anchor["from_ln"].astype(BF16)
                    h_c = jnp.concatenate([top, h[2 * SUBLANES:, :]], axis=0)
                a = jnp.dot(h_c, w1_ref[:, f0:f0 + f_chunk], preferred_element_type=F32)
                b = jnp.dot(h_c, w3_ref[:, f0:f0 + f_chunk], preferred_element_type=F32)
                g_chunks.append((_silu(a) * b).astype(BF16))
                norm_rows()
                if do_norm:
                    anchor["from_mm"] = a[0:1, 0:1] * 0.0
            acc = jnp.dot(jnp.concatenate(g_chunks, axis=1), w2_ref[...], preferred_element_type=F32)
        while ln_blocks:
            norm_rows()
        if do_matmul:
            y_ref[...] = alpha * x + (MACARON_HALF * gate) * acc

    g_idx = pl.program_id(0)
    pl.when(g_idx == 0)(functools.partial(step, True, False))
    pl.when(jnp.logical_and(g_idx > 0, g_idx < n_tiles))(functools.partial(step, True, True))
    pl.when(g_idx == n_tiles)(functools.partial(step, False, True))


def _ffn_call(x, mod, w1, w3, w2, ln_g, ln_b, *, layer, sub, alpha, tm, f_chunk):
    bsz, seq, d = x.shape
    mod_rows = mod.shape[0] // (w1.shape[0] * 3)
    tiles_per_seq = seq // tm
    n_tiles = bsz * tiles_per_seq

    def mm_tile(g):
        return jnp.minimum(g, n_tiles - 1)

    def ln_tile(g):
        return jnp.maximum(g - 1, 0)

    def x_index(t):
        return (t // tiles_per_seq, t % tiles_per_seq, 0)

    return pl.pallas_call(
        functools.partial(_ffn_kernel, alpha=alpha, f_chunk=f_chunk, n_tiles=n_tiles),
        grid=(n_tiles + 1,),
        in_specs=[
            pl.BlockSpec((1, tm, d), lambda g: x_index(mm_tile(g))),
            pl.BlockSpec((None, 3, d), lambda g: ((layer * mod_rows + mm_tile(g) // tiles_per_seq) * 3 + sub, 0, 0)),
            _layer_spec(w1.shape, layer),
            _layer_spec(w3.shape, layer),
            _layer_spec(w2.shape, layer),
            _layer_spec(ln_g.shape, layer * 3 + sub),
            _layer_spec(ln_b.shape, layer * 3 + sub),
        ],
        out_specs=pl.BlockSpec((1, tm, d), lambda g: x_index(ln_tile(g))),
        out_shape=jax.ShapeDtypeStruct(x.shape, F32),
        scratch_shapes=[pltpu.VMEM((tm, d), F32)],
        compiler_params=pltpu.CompilerParams(
            dimension_semantics=("arbitrary",),
            vmem_limit_bytes=VMEM_LIMIT_BYTES),
    )(x, mod, w1, w3, w2, ln_g, ln_b)


_ZT_STEP_RE = 0
_ZT_STEP_IM = SUBLANES
_ZT_HS_RE = 2 * SUBLANES
_ZT_HS_IM = 2 * SUBLANES + 3
_ZT_ROWS = 2 * SUBLANES + 6
_HS_SHIFTS = (1, 2, 4)


def _swap_halves(v, lane_in_head, half):
    n = v.shape[-1]
    fwd = pltpu.roll(v, half, 1)
    bwd = pltpu.roll(v, n - half, 1)
    return jnp.where(lane_in_head < half, bwd, fwd)


def _row_bcast(v, row):
    return jnp.broadcast_to(v[row:row + 1, :], v.shape)


_N_MIXER_IO = 28


def _mixer_kernel(*refs, **static):
    io, (z_a, zs_a, z_b, zs_b), rest = refs[:_N_MIXER_IO], refs[_N_MIXER_IO:_N_MIXER_IO + 4], refs[_N_MIXER_IO + 4:]
    n_tiles = static.pop("n_tiles")
    g_idx = pl.program_id(0)
    inner = jnp.logical_and(g_idx > 0, g_idx < n_tiles)
    even = (z_a, zs_a, z_b, zs_b)
    odd = (z_b, zs_b, z_a, zs_a)

    def run(bufs, **phases):
        return functools.partial(_mixer_step, *io, *bufs, *rest, **static, **phases)

    pl.when(g_idx == 0)(run(even, do_proj=True, do_mix=False))
    pl.when(jnp.logical_and(inner, g_idx % 2 == 0))(run(even, do_proj=True, do_mix=True))
    pl.when(jnp.logical_and(inner, g_idx % 2 == 1))(run(odd, do_proj=True, do_mix=True))
    pl.when(g_idx == n_tiles)(run(even if n_tiles % 2 == 0 else odd, do_proj=False, do_mix=True))


def _mixer_step(
        xp_ref, modp_ref,
        x_ref, pos_ref, mod_ref, w_in_ref, conv_w_ref, conv_b_ref, wg_ref, bg_ref, lam_ref,
        lg_ref, gn_g_ref, gn_b_ref, avg_ref,
        s5_lr_ref, s5_li_ref, s5_dt_ref, s5_bre_ref, s5_bim_ref, s5_c_ref, s5_d_ref,
        wglu_ref, bglu_ref, w_out_ref, lng_ref, lnb_ref,
        o_ref,
        zp_ref, zsp_ref, zc_ref, zsc_ref,
        ycat_ref, tail_ref, hcar_ref, bu_ref, xb_ref, xcar_ref, ztab_ref, bbar_ref,
        dmask_ref, qdec_ref, kdec_ref, cdec_ref, rstate_ref,
        *, alpha, tile, tiles_per_seq, d_lru, d_ret, d_ssm, do_proj, do_mix):
    g_idx = pl.program_id(0)
    n_state = bu_ref.shape[1] // 2
    n_blk = tile // SCAN_BLOCK
    lru_slabs = d_lru // LANES
    ssm_slabs = d_ssm // LANES
    ret_pairs = d_ret // LANES
    head_dim = d_ret // RET_HEADS
    half = head_dim // 2

    def _init_tables():
        lr = s5_lr_ref[...]
        li = s5_li_ref[...]
        dt = jnp.exp(s5_dt_ref[...])
        sub = lax.broadcasted_iota(jnp.int32, (SUBLANES, n_state), 0)

        def z_power(n):
            mag = jnp.exp(n * (lr * dt))
            ang = n * (li * dt)
            return mag * jnp.cos(ang), mag * jnp.sin(ang)

        st_r, st_i = z_power((sub + 1).astype(F32))
        ch_r, ch_i = z_power((SUBLANES * (sub + 1)).astype(F32))
        for b in range(SUBLANES):
            ztab_ref[_ZT_STEP_RE + b] = _row_bcast(st_r, b)
            ztab_ref[_ZT_STEP_IM + b] = _row_bcast(st_i, b)
        for n, d in enumerate(_HS_SHIFTS):
            ztab_ref[_ZT_HS_RE + n] = jnp.where(sub >= d, _row_bcast(ch_r, d - 1), 0.0)
            ztab_ref[_ZT_HS_IM + n] = jnp.where(sub >= d, _row_bcast(ch_i, d - 1), 0.0)
        zr = st_r[0:1, :]
        zi = st_i[0:1, :]
        den = lr * lr + li * li
        er = ((zr - 1.0) * lr + zi * li) / den
        ei = (zi * lr - (zr - 1.0) * li) / den
        bre = s5_bre_ref[...]
        bim = s5_bim_ref[...]
        bbar_ref[:, 0:n_state] = (er * bre - ei * bim).astype(BF16)
        bbar_ref[:, n_state:2 * n_state] = (er * bim + ei * bre).astype(BF16)
        tq = lax.broadcasted_iota(jnp.int32, (RET_CHUNK, 2 * RET_CHUNK), 0)
        tk = lax.broadcasted_iota(jnp.int32, (RET_CHUNK, 2 * RET_CHUNK), 1) % RET_CHUNK
        diff = (tq - tk).astype(F32)
        row = lax.broadcasted_iota(jnp.int32, (RET_CHUNK, d_ret), 0).astype(F32)
        lg_row = lg_ref[...]
        qdec_ref[...] = jnp.exp(lg_row * (row + 1.0))
        kdec_ref[...] = jnp.exp(lg_row * (RET_CHUNK - 1.0 - row))
        cdec_ref[...] = jnp.exp(lg_row * float(RET_CHUNK))
        for p in range(ret_pairs):
            lg_lo = lg_ref[0:1, p * LANES:p * LANES + 1]
            lg_hi = lg_ref[0:1, p * LANES + head_dim:p * LANES + head_dim + 1]
            col = lax.broadcasted_iota(jnp.int32, (RET_CHUNK, 2 * RET_CHUNK), 1)
            lg_pair = jnp.where(col < RET_CHUNK, lg_lo, lg_hi)
            dmask_ref[p] = jnp.where(diff >= 0.0, jnp.exp(lg_pair * jnp.maximum(diff, 0.0)), 0.0)

    if not do_mix:
        _init_tables()

    if do_mix:
        @pl.when((g_idx - 1) % tiles_per_seq == 0)
        def _reset_state():
            tail_ref[...] = jnp.zeros(tail_ref.shape, F32)
            hcar_ref[...] = jnp.zeros(hcar_ref.shape, F32)
            xcar_ref[...] = jnp.zeros(xcar_ref.shape, F32)
            rstate_ref[...] = jnp.zeros(rstate_ref.shape, F32)

    o_lru, o_glru = 0, d_lru
    o_q, o_k, o_v, o_gret = 2 * d_lru, 2 * d_lru + d_ret, 2 * d_lru + 2 * d_ret, 2 * d_lru + 3 * d_ret
    o_ssm = 2 * d_lru + 4 * d_ret

    if do_proj:
        xp = xp_ref[0]
        hp = (xp * (1.0 + modp_ref[1:2, :]) + modp_ref[0:1, :]).astype(BF16)
    scan_cols = ([o_lru + i * LANES for i in range(lru_slabs)]
                 + [o_ssm + i * LANES for i in range(ssm_slabs)])
    proj_chunks = list(range(0, w_in_ref.shape[1], PROJ_CHUNK)) if do_proj else []

    def project(n_chunks):
        for _ in range(min(n_chunks, len(proj_chunks))):
            c0 = proj_chunks.pop(0)
            zchunk = jnp.dot(hp, w_in_ref[:, c0:c0 + PROJ_CHUNK], preferred_element_type=F32)
            zp_ref[:, c0:c0 + PROJ_CHUNK] = zchunk
            for s_idx, sc in enumerate(scan_cols):
                if c0 <= sc < c0 + PROJ_CHUNK:
                    zsp_ref[s_idx] = zchunk[:, sc - c0:sc - c0 + LANES]

    if not do_mix:
        project(len(proj_chunks))
        return

    x = x_ref[0]
    gate = mod_ref[2:3, :]

    def z_cols(c0, width, r0=0, n_rows=tile):
        return zc_ref[r0:r0 + n_rows, c0:c0 + width]

    def scan_rows(k, b):
        return pl.ds(k * SCAN_BLOCK + b, SUBLANES, stride=SUBLANES)

    def blk_rows(k, b):
        return slice(k * SCAN_BLOCK + b * SUBLANES, k * SCAN_BLOCK + (b + 1) * SUBLANES)

    u_rows = [[jnp.concatenate([zsc_ref[i, scan_rows(k, b), :] for i in range(lru_slabs)], axis=1)
               for b in range(SUBLANES)] for k in range(n_blk)]
    us_rows = [[jnp.concatenate([zsc_ref[lru_slabs + i, scan_rows(k, b), :] for i in range(ssm_slabs)], axis=1)
                for b in range(SUBLANES)] for k in range(n_blk)]

    sub_l = lax.broadcasted_iota(jnp.int32, (SUBLANES, d_lru), 0)
    sub_s = lax.broadcasted_iota(jnp.int32, (SUBLANES, n_state), 0)

    pos_row = pos_ref[0].astype(F32)
    freq = lax.broadcasted_iota(jnp.int32, (half, 1), 0).astype(F32)
    inv = jnp.exp(freq * (-math.log(ROPE_BASE) / half))
    ang_t = inv * pos_row
    cos_t = jnp.cos(ang_t)
    sin_t = jnp.sin(ang_t)
    reps = LANES // head_dim
    cos_tab = jnp.concatenate([cos_t, cos_t] * reps, axis=0).T
    sin_tab = jnp.concatenate([-sin_t, sin_t] * reps, axis=0).T
    lane_in_head = lax.broadcasted_iota(jnp.int32, (RET_CHUNK, LANES), 1) % head_dim
    lane = lax.broadcasted_iota(jnp.int32, (RET_CHUNK, LANES), 1)
    lo_mask = lane < head_dim
    blk_r = lax.broadcasted_iota(jnp.int32, (LANES, LANES), 0) // head_dim
    blk_c = lax.broadcasted_iota(jnp.int32, (LANES, LANES), 1) // head_dim
    same_head = blk_r == blk_c
    inv_sqrt_dh = head_dim ** -0.5
    ret_state = [rstate_ref[p] for p in range(ret_pairs)]
    ret_out = [[] for _ in range(ret_pairs)]
    ret_units = [(ci_, p) for ci_ in range(tile // RET_CHUNK) for p in range(ret_pairs)]

    def retention_units(n_units):
        for _ in range(min(n_units, len(ret_units))):
            ci_, p = ret_units.pop(0)
            rows = slice(ci_ * RET_CHUNK, (ci_ + 1) * RET_CHUNK)
            cols = slice(p * LANES, (p + 1) * LANES)
            cs = cos_tab[rows]
            sn = sin_tab[rows]
            q_c = z_cols(o_q + p * LANES, LANES, ci_ * RET_CHUNK, RET_CHUNK)
            k_c = z_cols(o_k + p * LANES, LANES, ci_ * RET_CHUNK, RET_CHUNK)
            v_c = z_cols(o_v + p * LANES, LANES, ci_ * RET_CHUNK, RET_CHUNK)
            qr = q_c * cs + _swap_halves(q_c, lane_in_head, half) * sn
            kr = (k_c * cs + _swap_halves(k_c, lane_in_head, half) * sn) * inv_sqrt_dh
            k_bd = jnp.concatenate([jnp.where(lo_mask, kr, 0.0), jnp.where(lo_mask, 0.0, kr)], axis=0)
            v_bd = jnp.concatenate([jnp.where(lo_mask, v_c, 0.0), jnp.where(lo_mask, 0.0, v_c)], axis=0)
            scores = lax.dot_general(qr.astype(BF16), k_bd.astype(BF16), (((1,), (1,)), ((), ())),
                                     preferred_element_type=F32)
            scores = scores * dmask_ref[p]
            o_c = jnp.dot(scores.astype(BF16), v_bd.astype(BF16), preferred_element_type=F32)
            o_c = o_c + jnp.dot((qr * qdec_ref[:, cols]).astype(BF16), ret_state[p].astype(BF16),
                                preferred_element_type=F32)
            kv = jnp.dot((kr * kdec_ref[:, cols]).T.astype(BF16), v_c.astype(BF16), preferred_element_type=F32)
            ret_state[p] = cdec_ref[:, cols] * ret_state[p] + jnp.where(same_head, kv, 0.0)
            ret_out[p].append(o_c)

    n_tail = CONV_WIDTH - 1
    uc_rows = []
    for k in range(n_blk):
        early = {}
        for dd in range(1, n_tail + 1):
            cur = u_rows[k][SUBLANES - dd]
            if k == 0:
                prev = tail_ref[(n_tail - dd) * SUBLANES:(n_tail - dd + 1) * SUBLANES, :]
            else:
                prev = u_rows[k - 1][SUBLANES - dd]
            early[-dd] = pltpu.roll(jnp.where(sub_l == SUBLANES - 1, prev, cur), 1, 0)
        for b in range(SUBLANES):
            acc = conv_b_ref[...]
            for kk in range(CONV_WIDTH):
                src = b - n_tail + kk
                acc = acc + (u_rows[k][src] if src >= 0 else early[src]) * conv_w_ref[kk:kk + 1, :]
            uc_rows.append(acc)
    for dd in range(1, n_tail + 1):
        tail_ref[(n_tail - dd) * SUBLANES:(n_tail - dd + 1) * SUBLANES, :] = u_rows[n_blk - 1][SUBLANES - dd]
    project(1)
    uc = jnp.concatenate(uc_rows, axis=0)
    uc_bf = uc.astype(BF16)
    r_parts, i_parts = [], []
    for c0 in range(0, d_lru, MXU_DIM):
        wid = min(MXU_DIM, d_lru - c0)
        gz = (jnp.dot(uc_bf[:, c0:c0 + wid], wg_ref[c0:c0 + wid, 2 * c0:2 * (c0 + wid)],
                      preferred_element_type=F32) + bg_ref[:, 2 * c0:2 * (c0 + wid)])
        r_parts.append(gz[:, 0:wid])
        i_parts.append(gz[:, wid:2 * wid])
    r_gate = jax.nn.sigmoid(jnp.concatenate(r_parts, axis=1))
    i_gate = jax.nn.sigmoid(jnp.concatenate(i_parts, axis=1))
    project(1)
    lam = lam_ref[...]
    softplus_neg_lam = jnp.maximum(-lam, 0.0) + jnp.log(1.0 + jnp.exp(-jnp.abs(lam)))
    a_coef = jnp.exp((-LRU_C * softplus_neg_lam) * r_gate)
    b_term = jnp.sqrt(1.0 - a_coef * a_coef) * (i_gate * uc)
    project(1)

    h_car = jnp.broadcast_to(hcar_ref[...], (SUBLANES, d_lru))
    for k in range(n_blk):
        if k % 4 == 0:
            project(1)
        h_prev = jnp.where(sub_l == 0, h_car, 0.0)
        h_loc, a_cum = [], []
        for b in range(SUBLANES):
            a_b = a_coef[blk_rows(k, b)]
            h_prev = a_b * h_prev + b_term[blk_rows(k, b)]
            h_loc.append(h_prev)
            a_cum.append(a_b if b == 0 else a_b * a_cum[-1])
        pa, pb = a_cum[-1], h_loc[-1]
        for d in _HS_SHIFTS:
            live = sub_l >= d
            pb = jnp.where(live, pa * pltpu.roll(pb, d, 0) + pb, pb)
            pa = jnp.where(live, pa * pltpu.roll(pa, d, 0), pa)
        h_in = jnp.where(sub_l == 0, 0.0, pltpu.roll(pb, 1, 0))
        h_car = _row_bcast(pb, SUBLANES - 1)
        for b in range(SUBLANES):
            h_t = h_loc[b] + a_cum[b] * h_in
            for i in range(lru_slabs):
                ycat_ref[i, scan_rows(k, b), :] = h_t[:, i * LANES:(i + 1) * LANES]
    hcar_ref[...] = h_car[0:1, :]

    us_p = jnp.concatenate([us_rows[k][b] for k in range(n_blk) for b in range(SUBLANES)], axis=0)
    bu_ref[...] = jnp.dot(us_p.astype(BF16), bbar_ref[...], preferred_element_type=F32)
    re_cols = slice(0, n_state)
    im_cols = slice(n_state, 2 * n_state)
    zr = ztab_ref[_ZT_STEP_RE]
    zi = ztab_ref[_ZT_STEP_IM]
    xc_r = jnp.broadcast_to(xcar_ref[0:1, :], (SUBLANES, n_state))
    xc_i = jnp.broadcast_to(xcar_ref[1:2, :], (SUBLANES, n_state))
    for k in range(n_blk):
        if k % 3 == 0:
            project(1)
        retention_units(2 - k % 2)
        xr = jnp.where(sub_s == 0, xc_r, 0.0)
        xi = jnp.where(sub_s == 0, xc_i, 0.0)
        for b in range(SUBLANES):
            xr, xi = (zr * xr - zi * xi + bu_ref[blk_rows(k, b), re_cols],
                      zr * xi + zi * xr + bu_ref[blk_rows(k, b), im_cols])
            bu_ref[blk_rows(k, b), re_cols] = xr
            bu_ref[blk_rows(k, b), im_cols] = xi
        pr, pi = xr, xi
        for n, d in enumerate(_HS_SHIFTS):
            hr, hi = ztab_ref[_ZT_HS_RE + n], ztab_ref[_ZT_HS_IM + n]
            sr, si = pltpu.roll(pr, d, 0), pltpu.roll(pi, d, 0)
            pr, pi = pr + hr * sr - hi * si, pi + hr * si + hi * sr
        xin_r = jnp.where(sub_s == 0, 0.0, pltpu.roll(pr, 1, 0))
        xin_i = jnp.where(sub_s == 0, 0.0, pltpu.roll(pi, 1, 0))
        xc_r = _row_bcast(pr, SUBLANES - 1)
        xc_i = _row_bcast(pi, SUBLANES - 1)
        for b0 in range(0, SUBLANES, 2):
            halves_r, halves_i = [], []
            for b in (b0, b0 + 1):
                sr, si = ztab_ref[_ZT_STEP_RE + b], ztab_ref[_ZT_STEP_IM + b]
                halves_r.append(bu_ref[blk_rows(k, b), re_cols] + sr * xin_r - si * xin_i)
                halves_i.append(bu_ref[blk_rows(k, b), im_cols] + sr * xin_i + si * xin_r)
            rows2 = slice(k * SCAN_BLOCK + b0 * SUBLANES, k * SCAN_BLOCK + (b0 + 2) * SUBLANES)
            xb_ref[rows2, re_cols] = jnp.concatenate(halves_r, axis=0).astype(BF16)
            xb_ref[rows2, im_cols] = jnp.concatenate(halves_i, axis=0).astype(BF16)
    xcar_ref[0:1, :] = xc_r[0:1, :]
    xcar_ref[1:2, :] = xc_i[0:1, :]

    y_s = jnp.dot(xb_ref[...], s5_c_ref[...], preferred_element_type=F32)
    y_s = jax.nn.gelu(y_s + s5_d_ref[...] * us_p)
    glu = jnp.dot(y_s.astype(BF16), wglu_ref[...], preferred_element_type=F32) + bglu_ref[...]
    y_ssm = y_s * jax.nn.sigmoid(glu)
    for k in range(n_blk):
        for b in range(SUBLANES):
            for i in range(ssm_slabs):
                ycat_ref[lru_slabs + i, scan_rows(k, b), :] = y_ssm[blk_rows(k, b), i * LANES:(i + 1) * LANES]

    retention_units(len(ret_units))
    for p in range(ret_pairs):
        rstate_ref[p] = ret_state[p]
    o_all = jnp.concatenate([jnp.concatenate(ret_out[p], axis=0) for p in range(ret_pairs)], axis=1)
    def head_mean(v):
        parts = []
        for c0 in range(0, d_ret, MXU_DIM):
            c1 = min(c0 + MXU_DIM, d_ret)
            parts.append(jnp.dot(v[:, c0:c1].astype(BF16), avg_ref[c0:c1, c0:c1], preferred_element_type=F32))
        return jnp.concatenate(parts, axis=1)

    mu = head_mean(o_all)
    dev = o_all - mu
    d2 = dev * dev
    var = head_mean(d2)
    o_n = dev * lax.rsqrt(var + LN_EPS) * gn_g_ref[...] + gn_b_ref[...]
    y_ret = _silu(z_cols(o_gret, d_ret)) * o_n

    y_lru = [ycat_ref[i] * jax.nn.gelu(z_cols(o_glru + i * LANES, LANES))
             for i in range(lru_slabs)]
    y_ssm_t = [ycat_ref[lru_slabs + i] for i in range(ssm_slabs)]
    ycat = jnp.concatenate([v.astype(BF16) for v in y_lru] + [y_ret.astype(BF16)]
                           + [v.astype(BF16) for v in y_ssm_t], axis=1)
    m = jnp.dot(ycat, w_out_ref[...], preferred_element_type=F32)
    y = alpha * x + gate * m
    n_ln_blocks = max(1, len(proj_chunks))
    ln_rows = tile // n_ln_blocks
    for r in range(n_ln_blocks):
        rows = slice(r * ln_rows, (r + 1) * ln_rows)
        o_ref[0, rows, :] = _layer_norm_rows(y[rows], lng_ref[...], lnb_ref[...])
        project(1)


def _mixer_call(x, pos3, mod, prm, ln_g, ln_b, *, layer, alpha, tile):
    bsz, seq, d = x.shape
    depth = prm["lam"].shape[0]
    d_lru = prm["lam"].shape[2]
    d_ret = prm["gn_g"].shape[2]
    d_ssm = prm["s5_d"].shape[2]
    n_state = prm["s5_lr"].shape[2]
    n_in = prm["w_in"].shape[2]
    n_scan_slabs = (d_lru + d_ssm) // LANES
    tiles_per_seq = seq // tile
    n_tiles = bsz * tiles_per_seq
    mod_rows = mod.shape[0] // (depth * 3)
    names = ["w_in", "conv_w", "conv_b", "wg", "bg", "lam", "lg", "gn_g", "gn_b", "avg",
             "s5_lr", "s5_li", "s5_dt", "s5_bre", "s5_bim", "s5_c", "s5_d", "wglu", "bglu", "w_out"]
    consts = [prm[k] for k in names]
    scratch = [
        pltpu.VMEM((tile, n_in), F32),
        pltpu.VMEM((n_scan_slabs, tile, LANES), F32),
        pltpu.VMEM((tile, n_in), F32),
        pltpu.VMEM((n_scan_slabs, tile, LANES), F32),
        pltpu.VMEM((n_scan_slabs, tile, LANES), F32),
        pltpu.VMEM(((CONV_WIDTH - 1) * SUBLANES, d_lru), F32),
        pltpu.VMEM((1, d_lru), F32),
        pltpu.VMEM((tile, 2 * n_state), F32),
        pltpu.VMEM((tile, 2 * n_state), BF16),
        pltpu.VMEM((2, n_state), F32),
        pltpu.VMEM((_ZT_ROWS, SUBLANES, n_state), F32),
        pltpu.VMEM((d_ssm, 2 * n_state), BF16),
        pltpu.VMEM((d_ret // LANES, RET_CHUNK, 2 * RET_CHUNK), F32),
        pltpu.VMEM((RET_CHUNK, d_ret), F32),
        pltpu.VMEM((RET_CHUNK, d_ret), F32),
        pltpu.VMEM((1, d_ret), F32),
        pltpu.VMEM((d_ret // LANES, LANES, LANES), F32),
    ]

    def proj_tile(g):
        return jnp.minimum(g, n_tiles - 1)

    def mix_tile(g):
        return jnp.maximum(g - 1, 0)

    def x_index(t):
        return (t // tiles_per_seq, t % tiles_per_seq, 0)

    def mod_index(t):
        return ((layer * mod_rows + t // tiles_per_seq) * 3 + 1, 0, 0)

    return pl.pallas_call(
        functools.partial(_mixer_kernel, alpha=alpha, tile=tile, tiles_per_seq=tiles_per_seq,
                          n_tiles=n_tiles, d_lru=d_lru, d_ret=d_ret, d_ssm=d_ssm),
        grid=(n_tiles + 1,),
        in_specs=[
            pl.BlockSpec((1, tile, d), lambda g: x_index(proj_tile(g))),
            pl.BlockSpec((None, 3, d), lambda g: mod_index(proj_tile(g))),
            pl.BlockSpec((1, tile, d), lambda g: x_index(mix_tile(g))),
            pl.BlockSpec((1, 1, tile), lambda g: (mix_tile(g), 0, 0)),
            pl.BlockSpec((None, 3, d), lambda g: mod_index(mix_tile(g))),
        ] + [_layer_spec(a.shape, layer) for a in consts]
          + [_layer_spec(ln_g.shape, layer * 3 + 1), _layer_spec(ln_b.shape, layer * 3 + 1)],
        out_specs=pl.BlockSpec((1, tile, d), lambda g: x_index(mix_tile(g))),
        out_shape=jax.ShapeDtypeStruct(x.shape, F32),
        scratch_shapes=scratch,
        compiler_params=pltpu.CompilerParams(
            dimension_semantics=("arbitrary",),
            vmem_limit_bytes=VMEM_LIMIT_BYTES),
    )(x, mod, x, pos3, mod, *consts, ln_g, ln_b)


def _block_diag(blocks):
    depth, n, r, c = blocks.shape
    eye = jnp.eye(n, dtype=blocks.dtype)
    return (eye[None, :, None, :, None] * blocks[:, :, :, None, :]).reshape(depth, n * r, n * c)


def _gate_groups(a, b):
    n = a.shape[-1]
    parts = []
    for c0 in range(0, n, MXU_DIM):
        parts += [a[..., c0:c0 + MXU_DIM], b[..., c0:c0 + MXU_DIM]]
    return jnp.concatenate(parts, axis=-1)


def _mixer_params(mix_w_in, conv_w, conv_b, lru_wa, lru_ba, lru_wx, lru_bx, lru_lam,
                  ret_gn_g, ret_gn_b, ssm_lam_re, ssm_lam_im, ssm_log_step, ssm_b_re, ssm_b_im,
                  ssm_c_re, ssm_c_im, ssm_d, ssm_w_glu, ssm_b_glu, mix_w_out):
    depth, d_lru = lru_lam.shape
    d_ret = ret_gn_g.shape[1]
    d_ssm = ssm_d.shape[1]
    groups, n_per = ssm_lam_re.shape[1], ssm_lam_re.shape[2]
    n_state = groups * n_per
    head_dim = d_ret // RET_HEADS
    log_gamma = np.array([math.log1p(-2.0 ** (-5.0 - hh)) for hh in range(RET_HEADS)], np.float64)
    lg_lane = np.repeat(log_gamma, head_dim)[None, None, :].repeat(depth, axis=0)
    avg = np.kron(np.eye(RET_HEADS), np.full((head_dim, head_dim), 1.0 / head_dim))[None].repeat(depth, axis=0)
    c_re_bd = _block_diag(jnp.swapaxes(ssm_c_re, 2, 3))
    c_im_bd = _block_diag(jnp.swapaxes(ssm_c_im, 2, 3))
    return {
        "w_in": mix_w_in.astype(BF16),
        "conv_w": conv_w,
        "conv_b": conv_b.reshape(depth, 1, d_lru),
        "wg": _gate_groups(_block_diag(lru_wa), _block_diag(lru_wx)).astype(BF16),
        "bg": _gate_groups(lru_ba[:, None, :], lru_bx[:, None, :]),
        "lam": lru_lam.reshape(depth, 1, d_lru),
        "lg": jnp.asarray(lg_lane, F32),
        "gn_g": ret_gn_g.reshape(depth, 1, d_ret),
        "gn_b": ret_gn_b.reshape(depth, 1, d_ret),
        "avg": jnp.asarray(avg, BF16),
        "s5_lr": ssm_lam_re.reshape(depth, 1, n_state),
        "s5_li": ssm_lam_im.reshape(depth, 1, n_state),
        "s5_dt": jnp.repeat(ssm_log_step, n_per, axis=1).reshape(depth, 1, n_state),
        "s5_bre": _block_diag(jnp.swapaxes(ssm_b_re, 2, 3)),
        "s5_bim": _block_diag(jnp.swapaxes(ssm_b_im, 2, 3)),
        "s5_c": jnp.concatenate([c_re_bd, -c_im_bd], axis=1).astype(BF16),
        "s5_d": ssm_d.reshape(depth, 1, d_ssm),
        "wglu": ssm_w_glu.astype(BF16),
        "bglu": ssm_b_glu.reshape(depth, 1, d_ssm),
        "w_out": mix_w_out.astype(BF16),
    }


def _tiles(seq):
    tm = 512 if seq % 512 == 0 else seq
    tile = 512 if seq % 512 == 0 else seq
    return tm, tile


def kernel(x, c, positions, ada_w, ada_b, ln_g, ln_b, ffn1_w1, ffn1_w3, ffn1_w2, mix_w_in, conv_w, conv_b, lru_wa, lru_ba, lru_wx, lru_bx, lru_lam, ret_gn_g, ret_gn_b, ssm_lam_re, ssm_lam_im, ssm_log_step, ssm_b_re, ssm_b_im, ssm_c_re, ssm_c_im, ssm_d, ssm_w_glu, ssm_b_glu, mix_w_out, ffn2_w1, ffn2_w3, ffn2_w2):
    bsz, seq, d = x.shape
    depth = ada_w.shape[0]
    alpha = (2.0 * depth) ** 0.25
    tm, tile = _tiles(seq)
    f_chunk = 256

    rows = -(-bsz // SUBLANES) * SUBLANES
    c_pad = jnp.pad(c, ((0, rows - bsz), (0, 0)))
    mod = _ada_call(c_pad, ada_w, ada_b).reshape(depth * rows * 3, 3, d)
    pos3 = positions.reshape(bsz * (seq // tile), 1, tile)
    lng = ln_g.reshape(depth * 3, 1, d)
    lnb = ln_b.reshape(depth * 3, 1, d)
    ffn1 = (ffn1_w1.astype(BF16), ffn1_w3.astype(BF16), ffn1_w2.astype(BF16))
    ffn2 = (ffn2_w1.astype(BF16), ffn2_w3.astype(BF16), ffn2_w2.astype(BF16))
    prm = _mixer_params(mix_w_in, conv_w, conv_b, lru_wa, lru_ba, lru_wx, lru_bx, lru_lam,
                        ret_gn_g, ret_gn_b, ssm_lam_re, ssm_lam_im, ssm_log_step, ssm_b_re, ssm_b_im,
                        ssm_c_re, ssm_c_im, ssm_d, ssm_w_glu, ssm_b_glu, mix_w_out)

    for l in range(depth):
        x = _ffn_call(x, mod, *ffn1, lng, lnb, layer=l, sub=0, alpha=alpha, tm=tm, f_chunk=f_chunk)
        x = _mixer_call(x, pos3, mod, prm, lng, lnb, layer=l, alpha=alpha, tile=tile)
        x = _ffn_call(x, mod, *ffn2, lng, lnb, layer=l, sub=2, alpha=alpha, tm=tm, f_chunk=f_chunk)
    return x
```

```python
import functools
import math

import numpy as np
import jax
import jax.numpy as jnp
from jax import lax
from jax.experimental import pallas as pl
from jax.experimental.pallas import tpu as pltpu

F32 = jnp.float32
BF16 = jnp.bfloat16

LRU_HEADS = 6
CONV_WIDTH = 4
LRU_C = 8.0
RET_HEADS = 6
RET_CHUNK = 128
ROPE_BASE = 10000.0
SSM_GROUP = 16
SSM_STATE = 64
N_MOD = 9
MACARON_HALF = 0.5
LN_EPS = 1e-5

LANES = 128
SUBLANES = 8
VMEM_LIMIT_BYTES = 56 * 1024 * 1024
SCAN_BLOCK = SUBLANES * SUBLANES
MXU_DIM = 256
PROJ_CHUNK = MXU_DIM


def _silu(v):
    return v * jax.nn.sigmoid(v)


def _layer_norm_rows(v, g, b):
    mu = jnp.mean(v, axis=-1, keepdims=True)
    d = v - mu
    var = jnp.mean(d * d, axis=-1, keepdims=True)
    return d * lax.rsqrt(var + LN_EPS) * g + b


def _layer_spec(shape, layer):
    tail = (0,) * (len(shape) - 1)
    return pl.BlockSpec((None,) + tuple(shape[1:]), lambda *_: (layer,) + tail,
                        pipeline_mode=pl.Buffered(1))


def _ada_kernel(c_ref, w_ref, b_ref, o_ref):
    cond = _silu(c_ref[...]).astype(BF16)
    o_ref[0] = jnp.dot(cond, w_ref[0].astype(BF16), preferred_element_type=F32) + b_ref[0]


def _ada_call(c_pad, ada_w, ada_b):
    depth, d, n = ada_w.shape
    rows = c_pad.shape[0]
    tn = n // N_MOD
    return pl.pallas_call(
        _ada_kernel,
        grid=(depth, n // tn),
        in_specs=[
            pl.BlockSpec((rows, d), lambda l, j: (0, 0)),
            pl.BlockSpec((1, d, tn), lambda l, j: (l, 0, j)),
            pl.BlockSpec((1, 1, tn), lambda l, j: (l, 0, j)),
        ],
        out_specs=pl.BlockSpec((1, rows, tn), lambda l, j: (l, 0, j)),
        out_shape=jax.ShapeDtypeStruct((depth, rows, n), F32),
        compiler_params=pltpu.CompilerParams(
            dimension_semantics=("arbitrary", "arbitrary"),
            vmem_limit_bytes=VMEM_LIMIT_BYTES),
    )(c_pad, ada_w, ada_b.reshape(depth, 1, n))


def _ffn_kernel(x_ref, mod_ref, w1_ref, w3_ref, w2_ref, lng_ref, lnb_ref, o_ref, y_ref,
                *, alpha, f_chunk, n_tiles):
    tm = x_ref.shape[1]
    d_ff = w1_ref.shape[1]
    f_starts = list(range(0, d_ff, f_chunk))
    ln_rows = tm // SUBLANES

    def step(do_matmul, do_norm):
        ln_blocks = list(range(0, tm, ln_rows)) if do_norm else []
        anchor = {"from_mm": None, "from_ln": None}

        def norm_rows():
            if ln_blocks:
                r0 = ln_blocks.pop(0)
                y_blk = y_ref[r0:r0 + ln_rows, :]
                if anchor["from_mm"] is not None:
                    y_blk = y_blk + anchor["from_mm"]
                out = _layer_norm_rows(y_blk, lng_ref[...], lnb_ref[...])
                o_ref[0, r0:r0 + ln_rows, :] = out
                anchor["from_ln"] = out[0:1, :] * 0.0

        if do_matmul:
            x = x_ref[0]
            shift = mod_ref[0:1, :]
            scale = mod_ref[1:2, :]
            gate = mod_ref[2:3, :]
            h = (x * (1.0 + scale) + shift).astype(BF16)
            g_chunks = []
            for f0 in f_starts:
                h_c = h
                if anchor["from_ln"] is not None:
                    top = h[0:2 * SUBLANES, :] + anchor["from_ln"].astype(BF16)
                    h_c = jnp.concatenate([top, h[2 * SUBLANES:, :]], axis=0)
                a = jnp.dot(h_c, w1_ref[:, f0:f0 + f_chunk].astype(BF16), preferred_element_type=F32)
                b = jnp.dot(h_c, w3_ref[:, f0:f0 + f_chunk].astype(BF16), preferred_element_type=F32)
                g_chunks.append((_silu(a) * b).astype(BF16))
                norm_rows()
                if do_norm:
                    anchor["from_mm"] = a[0:1, 0:1] * 0.0
            acc = jnp.dot(jnp.concatenate(g_chunks, axis=1), w2_ref[...].astype(BF16),
                          preferred_element_type=F32)
        while ln_blocks:
            norm_rows()
        if do_matmul:
            y_ref[...] = alpha * x + (MACARON_HALF * gate) * acc

    g_idx = pl.program_id(0)
    pl.when(g_idx == 0)(functools.partial(step, True, False))
    pl.when(jnp.logical_and(g_idx > 0, g_idx < n_tiles))(functools.partial(step, True, True))
    pl.when(g_idx == n_tiles)(functools.partial(step, False, True))


def _ffn_call(x, mod, w1, w3, w2, ln_g, ln_b, *, layer, sub, alpha, tm, f_chunk):
    bsz, seq, d = x.shape
    mod_rows = mod.shape[0] // (w1.shape[0] * 3)
    tiles_per_seq = seq // tm
    n_tiles = bsz * tiles_per_seq

    def mm_tile(g):
        return jnp.minimum(g, n_tiles - 1)

    def ln_tile(g):
        return jnp.maximum(g - 1, 0)

    def x_index(t):
        return (t // tiles_per_seq, t % tiles_per_seq, 0)

    return pl.pallas_call(
        functools.partial(_ffn_kernel, alpha=alpha, f_chunk=f_chunk, n_tiles=n_tiles),
        grid=(n_tiles + 1,),
        in_specs=[
            pl.BlockSpec((1, tm, d), lambda g: x_index(mm_tile(g))),
            pl.BlockSpec((None, 3, d), lambda g: ((layer * mod_rows + mm_tile(g) // tiles_per_seq) * 3 + sub, 0, 0)),
            _layer_spec(w1.shape, layer),
            _layer_spec(w3.shape, layer),
            _layer_spec(w2.shape, layer),
            _layer_spec(ln_g.shape, layer * 3 + sub),
            _layer_spec(ln_b.shape, layer * 3 + sub),
        ],
        out_specs=pl.BlockSpec((1, tm, d), lambda g: x_index(ln_tile(g))),
        out_shape=jax.ShapeDtypeStruct(x.shape, F32),
        scratch_shapes=[pltpu.VMEM((tm, d), F32)],
        compiler_params=pltpu.CompilerParams(
            dimension_semantics=("arbitrary",),
            vmem_limit_bytes=VMEM_LIMIT_BYTES),
    )(x, mod, w1, w3, w2, ln_g, ln_b)


_ZT_STEP_RE = 0
_ZT_STEP_IM = SUBLANES
_ZT_HS_RE = 2 * SUBLANES
_ZT_HS_IM = 2 * SUBLANES + 3
_ZT_ROWS = 2 * SUBLANES + 6
_HS_SHIFTS = (1, 2, 4)


def _swap_halves(v, lane_in_head, half):
    n = v.shape[-1]
    fwd = pltpu.roll(v, half, 1)
    bwd = pltpu.roll(v, n - half, 1)
    return jnp.where(lane_in_head < half, bwd, fwd)


def _row_bcast(v, row):
    return jnp.broadcast_to(v[row:row + 1, :], v.shape)


_N_MIXER_IO = 28


def _mixer_kernel(*refs, **static):
    io, (z_a, zs_a, z_b, zs_b), rest = refs[:_N_MIXER_IO], refs[_N_MIXER_IO:_N_MIXER_IO + 4], refs[_N_MIXER_IO + 4:]
    n_tiles = static.pop("n_tiles")
    g_idx = pl.program_id(0)
    inner = jnp.logical_and(g_idx > 0, g_idx < n_tiles)
    even = (z_a, zs_a, z_b, zs_b)
    odd = (z_b, zs_b, z_a, zs_a)

    def run(bufs, **phases):
        return functools.partial(_mixer_step, *io, *bufs, *rest, **static, **phases)

    pl.when(g_idx == 0)(run(even, do_proj=True, do_mix=False))
    pl.when(jnp.logical_and(inner, g_idx % 2 == 0))(run(even, do_proj=True, do_mix=True))
    pl.when(jnp.logical_and(inner, g_idx % 2 == 1))(run(odd, do_proj=True, do_mix=True))
    pl.when(g_idx == n_tiles)(run(even if n_tiles % 2 == 0 else odd, do_proj=False, do_mix=True))


def _mixer_step(
        xp_ref, modp_ref,
        x_ref, pos_ref, mod_ref, w_in_ref, conv_w_ref, conv_b_ref, wg_ref, bg_ref, lam_ref,
        lg_ref, gn_g_ref, gn_b_ref, avg_ref,
        s5_lr_ref, s5_li_ref, s5_dt_ref, s5_bre_ref, s5_bim_ref, s5_c_ref, s5_d_ref,
        wglu_ref, bglu_ref, w_out_ref, lng_ref, lnb_ref,
        o_ref,
        zp_ref, zsp_ref, zc_ref, zsc_ref,
        ycat_ref, tail_ref, hcar_ref, bu_ref, xb_ref, xcar_ref, ztab_ref, bbar_ref,
        dmask_ref, qdec_ref, kdec_ref, cdec_ref, rstate_ref,
        *, alpha, tile, tiles_per_seq, d_lru, d_ret, d_ssm, do_proj, do_mix):
    g_idx = pl.program_id(0)
    n_state = bu_ref.shape[1] // 2
    n_blk = tile // SCAN_BLOCK
    lru_slabs = d_lru // LANES
    ssm_slabs = d_ssm // LANES
    ret_pairs = d_ret // LANES
    head_dim = d_ret // RET_HEADS
    half = head_dim // 2

    def _init_tables():
        lr = s5_lr_ref[...]
        li = s5_li_ref[...]
        dt = jnp.exp(s5_dt_ref[...])
        sub = lax.broadcasted_iota(jnp.int32, (SUBLANES, n_state), 0)

        def z_power(n):
            mag = jnp.exp(n * (lr * dt))
            ang = n * (li * dt)
            return mag * jnp.cos(ang), mag * jnp.sin(ang)

        st_r, st_i = z_power((sub + 1).astype(F32))
        ch_r, ch_i = z_power((SUBLANES * (sub + 1)).astype(F32))
        for b in range(SUBLANES):
            ztab_ref[_ZT_STEP_RE + b] = _row_bcast(st_r, b)
            ztab_ref[_ZT_STEP_IM + b] = _row_bcast(st_i, b)
        for n, d in enumerate(_HS_SHIFTS):
            ztab_ref[_ZT_HS_RE + n] = jnp.where(sub >= d, _row_bcast(ch_r, d - 1), 0.0)
            ztab_ref[_ZT_HS_IM + n] = jnp.where(sub >= d, _row_bcast(ch_i, d - 1), 0.0)
        zr = st_r[0:1, :]
        zi = st_i[0:1, :]
        den = lr * lr + li * li
        er = ((zr - 1.0) * lr + zi * li) / den
        ei = (zi * lr - (zr - 1.0) * li) / den
        bre = s5_bre_ref[...]
        bim = s5_bim_ref[...]
        bbar_ref[:, 0:n_state] = (er * bre - ei * bim).astype(BF16)
        bbar_ref[:, n_state:2 * n_state] = (er * bim + ei * bre).astype(BF16)
        tq = lax.broadcasted_iota(jnp.int32, (RET_CHUNK, 2 * RET_CHUNK), 0)
        tk = lax.broadcasted_iota(jnp.int32, (RET_CHUNK, 2 * RET_CHUNK), 1) % RET_CHUNK
        diff = (tq - tk).astype(F32)
        row = lax.broadcasted_iota(jnp.int32, (RET_CHUNK, d_ret), 0).astype(F32)
        lg_row = lg_ref[...]
        qdec_ref[...] = jnp.exp(lg_row * (row + 1.0))
        kdec_ref[...] = jnp.exp(lg_row * (RET_CHUNK - 1.0 - row))
        cdec_ref[...] = jnp.exp(lg_row * float(RET_CHUNK))
        for p in range(ret_pairs):
            lg_lo = lg_ref[0:1, p * LANES:p * LANES + 1]
            lg_hi = lg_ref[0:1, p * LANES + head_dim:p * LANES + head_dim + 1]
            col = lax.broadcasted_iota(jnp.int32, (RET_CHUNK, 2 * RET_CHUNK), 1)
            lg_pair = jnp.where(col < RET_CHUNK, lg_lo, lg_hi)
            dmask_ref[p] = jnp.where(diff >= 0.0, jnp.exp(lg_pair * jnp.maximum(diff, 0.0)), 0.0)

    if not do_mix:
        _init_tables()

    if do_mix:
        @pl.when((g_idx - 1) % tiles_per_seq == 0)
        def _reset_state():
            tail_ref[...] = jnp.zeros(tail_ref.shape, F32)
            hcar_ref[...] = jnp.zeros(hcar_ref.shape, F32)
            xcar_ref[...] = jnp.zeros(xcar_ref.shape, F32)
            rstate_ref[...] = jnp.zeros(rstate_ref.shape, F32)

    o_lru, o_glru = 0, d_lru
    o_q, o_k, o_v, o_gret = 2 * d_lru, 2 * d_lru + d_ret, 2 * d_lru + 2 * d_ret, 2 * d_lru + 3 * d_ret
    o_ssm = 2 * d_lru + 4 * d_ret

    if do_proj:
        xp = xp_ref[0]
        hp = (xp * (1.0 + modp_ref[1:2, :]) + modp_ref[0:1, :]).astype(BF16)
    scan_cols = ([o_lru + i * LANES for i in range(lru_slabs)]
                 + [o_ssm + i * LANES for i in range(ssm_slabs)])
    proj_chunks = list(range(0, w_in_ref.shape[1], PROJ_CHUNK)) if do_proj else []

    def project(n_chunks):
        for _ in range(min(n_chunks, len(proj_chunks))):
            c0 = proj_chunks.pop(0)
            zchunk = jnp.dot(hp, w_in_ref[:, c0:c0 + PROJ_CHUNK], preferred_element_type=F32)
            zp_ref[:, c0:c0 + PROJ_CHUNK] = zchunk
            for s_idx, sc in enumerate(scan_cols):
                if c0 <= sc < c0 + PROJ_CHUNK:
                    zsp_ref[s_idx] = zchunk[:, sc - c0:sc - c0 + LANES]

    if not do_mix:
        project(len(proj_chunks))
        return

    x = x_ref[0]
    gate = mod_ref[2:3, :]

    def z_cols(c0, width, r0=0, n_rows=tile):
        return zc_ref[r0:r0 + n_rows, c0:c0 + width]

    def scan_rows(k, b):
        return pl.ds(k * SCAN_BLOCK + b, SUBLANES, stride=SUBLANES)

    def blk_rows(k, b):
        return slice(k * SCAN_BLOCK + b * SUBLANES, k * SCAN_BLOCK + (b + 1) * SUBLANES)

    u_rows = [[jnp.concatenate([zsc_ref[i, scan_rows(k, b), :] for i in range(lru_slabs)], axis=1)
               for b in range(SUBLANES)] for k in range(n_blk)]
    us_rows = [[jnp.concatenate([zsc_ref[lru_slabs + i, scan_rows(k, b), :] for i in range(ssm_slabs)], axis=1)
                for b in range(SUBLANES)] for k in range(n_blk)]

    sub_l = lax.broadcasted_iota(jnp.int32, (SUBLANES, d_lru), 0)
    sub_s = lax.broadcasted_iota(jnp.int32, (SUBLANES, n_state), 0)

    pos_row = pos_ref[0].astype(F32)
    freq = lax.broadcasted_iota(jnp.int32, (half, 1), 0).astype(F32)
    inv = jnp.exp(freq * (-math.log(ROPE_BASE) / half))
    ang_t = inv * pos_row
    cos_t = jnp.cos(ang_t)
    sin_t = jnp.sin(ang_t)
    reps = LANES // head_dim
    cos_tab = jnp.concatenate([cos_t, cos_t] * reps, axis=0).T
    sin_tab = jnp.concatenate([-sin_t, sin_t] * reps, axis=0).T
    lane_in_head = lax.broadcasted_iota(jnp.int32, (RET_CHUNK, LANES), 1) % head_dim
    lane = lax.broadcasted_iota(jnp.int32, (RET_CHUNK, LANES), 1)
    lo_mask = lane < head_dim
    blk_r = lax.broadcasted_iota(jnp.int32, (LANES, LANES), 0) // head_dim
    blk_c = lax.broadcasted_iota(jnp.int32, (LANES, LANES), 1) // head_dim
    same_head = blk_r == blk_c
    inv_sqrt_dh = head_dim ** -0.5
    ret_state = [rstate_ref[p] for p in range(ret_pairs)]
    ret_out = [[] for _ in range(ret_pairs)]
    ret_units = [(ci_, p) for ci_ in range(tile // RET_CHUNK) for p in range(ret_pairs)]

    def retention_units(n_units):
        for _ in range(min(n_units, len(ret_units))):
            ci_, p = ret_units.pop(0)
            rows = slice(ci_ * RET_CHUNK, (ci_ + 1) * RET_CHUNK)
            cols = slice(p * LANES, (p + 1) * LANES)
            cs = cos_tab[rows]
            sn = sin_tab[rows]
            q_c = z_cols(o_q + p * LANES, LANES, ci_ * RET_CHUNK, RET_CHUNK)
            k_c = z_cols(o_k + p * LANES, LANES, ci_ * RET_CHUNK, RET_CHUNK)
            v_c = z_cols(o_v + p * LANES, LANES, ci_ * RET_CHUNK, RET_CHUNK)
            qr = q_c * cs + _swap_halves(q_c, lane_in_head, half) * sn
            kr = (k_c * cs + _swap_halves(k_c, lane_in_head, half) * sn) * inv_sqrt_dh
            k_bd = jnp.concatenate([jnp.where(lo_mask, kr, 0.0), jnp.where(lo_mask, 0.0, kr)], axis=0)
            v_bd = jnp.concatenate([jnp.where(lo_mask, v_c, 0.0), jnp.where(lo_mask, 0.0, v_c)], axis=0)
            scores = lax.dot_general(qr.astype(BF16), k_bd.astype(BF16), (((1,), (1,)), ((), ())),
                                     preferred_element_type=F32)
            scores = scores * dmask_ref[p]
            o_c = jnp.dot(scores.astype(BF16), v_bd.astype(BF16), preferred_element_type=F32)
            o_c = o_c + jnp.dot((qr * qdec_ref[:, cols]).astype(BF16), ret_state[p].astype(BF16),
                                preferred_element_type=F32)
            kv = jnp.dot((kr * kdec_ref[:, cols]).T.astype(BF16), v_c.astype(BF16), preferred_element_type=F32)
            ret_state[p] = cdec_ref[:, cols] * ret_state[p] + jnp.where(same_head, kv, 0.0)
            ret_out[p].append(o_c)

    n_tail = CONV_WIDTH - 1
    uc_rows = []
    for k in range(n_blk):
        early = {}
        for dd in range(1, n_tail + 1):
            cur = u_rows[k][SUBLANES - dd]
            if k == 0:
                prev = tail_ref[(n_tail - dd) * SUBLANES:(n_tail - dd + 1) * SUBLANES, :]
            else:
                prev = u_rows[k - 1][SUBLANES - dd]
            early[-dd] = pltpu.roll(jnp.where(sub_l == SUBLANES - 1, prev, cur), 1, 0)
        for b in range(SUBLANES):
            acc = conv_b_ref[...]
            for kk in range(CONV_WIDTH):
                src = b - n_tail + kk
                acc = acc + (u_rows[k][src] if src >= 0 else early[src]) * conv_w_ref[kk:kk + 1, :]
            uc_rows.append(acc)
    for dd in range(1, n_tail + 1):
        tail_ref[(n_tail - dd) * SUBLANES:(n_tail - dd + 1) * SUBLANES, :] = u_rows[n_blk - 1][SUBLANES - dd]
    project(1)
    uc = jnp.concatenate(uc_rows, axis=0)
    uc_bf = uc.astype(BF16)
    r_parts, i_parts = [], []
    for c0 in range(0, d_lru, MXU_DIM):
        wid = min(MXU_DIM, d_lru - c0)
        gz = (jnp.dot(uc_bf[:, c0:c0 + wid], wg_ref[c0:c0 + wid, 2 * c0:2 * (c0 + wid)],
                      preferred_element_type=F32) + bg_ref[:, 2 * c0:2 * (c0 + wid)])
        r_parts.append(gz[:, 0:wid])
        i_parts.append(gz[:, wid:2 * wid])
    r_gate = jax.nn.sigmoid(jnp.concatenate(r_parts, axis=1))
    i_gate = jax.nn.sigmoid(jnp.concatenate(i_parts, axis=1))
    project(1)
    lam = lam_ref[...]
    softplus_neg_lam = jnp.maximum(-lam, 0.0) + jnp.log(1.0 + jnp.exp(-jnp.abs(lam)))
    a_coef = jnp.exp((-LRU_C * softplus_neg_lam) * r_gate)
    b_term = jnp.sqrt(1.0 - a_coef * a_coef) * (i_gate * uc)
    project(1)

    h_car = jnp.broadcast_to(hcar_ref[...], (SUBLANES, d_lru))
    for k in range(n_blk):
        if k % 4 == 0:
            project(1)
        h_prev = jnp.where(sub_l == 0, h_car, 0.0)
        h_loc, a_cum = [], []
        for b in range(SUBLANES):
            a_b = a_coef[blk_rows(k, b)]
            h_prev = a_b * h_prev + b_term[blk_rows(k, b)]
            h_loc.append(h_prev)
            a_cum.append(a_b if b == 0 else a_b * a_cum[-1])
        pa, pb = a_cum[-1], h_loc[-1]
        for d in _HS_SHIFTS:
            live = sub_l >= d
            pb = jnp.where(live, pa * pltpu.roll(pb, d, 0) + pb, pb)
            pa = jnp.where(live, pa * pltpu.roll(pa, d, 0), pa)
        h_in = jnp.where(sub_l == 0, 0.0, pltpu.roll(pb, 1, 0))
        h_car = _row_bcast(pb, SUBLANES - 1)
        for b in range(SUBLANES):
            h_t = h_loc[b] + a_cum[b] * h_in
            for i in range(lru_slabs):
                ycat_ref[i, scan_rows(k, b), :] = h_t[:, i * LANES:(i + 1) * LANES]
    hcar_ref[...] = h_car[0:1, :]

    us_p = jnp.concatenate([us_rows[k][b] for k in range(n_blk) for b in range(SUBLANES)], axis=0)
    bu_ref[...] = jnp.dot(us_p.astype(BF16), bbar_ref[...], preferred_element_type=F32)
    re_cols = slice(0, n_state)
    im_cols = slice(n_state, 2 * n_state)
    zr = ztab_ref[_ZT_STEP_RE]
    zi = ztab_ref[_ZT_STEP_IM]
    xc_r = jnp.broadcast_to(xcar_ref[0:1, :], (SUBLANES, n_state))
    xc_i = jnp.broadcast_to(xcar_ref[1:2, :], (SUBLANES, n_state))
    for k in range(n_blk):
        if k % 3 == 0:
            project(1)
        retention_units(2 - k % 2)
        xr = jnp.where(sub_s == 0, xc_r, 0.0)
        xi = jnp.where(sub_s == 0, xc_i, 0.0)
        for b in range(SUBLANES):
            xr, xi = (zr * xr - zi * xi + bu_ref[blk_rows(k, b), re_cols],
                      zr * xi + zi * xr + bu_ref[blk_rows(k, b), im_cols])
            bu_ref[blk_rows(k, b), re_cols] = xr
            bu_ref[blk_rows(k, b), im_cols] = xi
        pr, pi = xr, xi
        for n, d in enumerate(_HS_SHIFTS):
            hr, hi = ztab_ref[_ZT_HS_RE + n], ztab_ref[_ZT_HS_IM + n]
            sr, si = pltpu.roll(pr, d, 0), pltpu.roll(pi, d, 0)
            pr, pi = pr + hr * sr - hi * si, pi + hr * si + hi * sr
        xin_r = jnp.where(sub_s == 0, 0.0, pltpu.roll(pr, 1, 0))
        xin_i = jnp.where(sub_s == 0, 0.0, pltpu.roll(pi, 1, 0))
        xc_r = _row_bcast(pr, SUBLANES - 1)
        xc_i = _row_bcast(pi, SUBLANES - 1)
        for b0 in range(0, SUBLANES, 2):
            halves_r, halves_i = [], []
            for b in (b0, b0 + 1):
                sr, si = ztab_ref[_ZT_STEP_RE + b], ztab_ref[_ZT_STEP_IM + b]
                halves_r.append(bu_ref[blk_rows(k, b), re_cols] + sr * xin_r - si * xin_i)
                halves_i.append(bu_ref[blk_rows(k, b), im_cols] + sr * xin_i + si * xin_r)
            rows2 = slice(k * SCAN_BLOCK + b0 * SUBLANES, k * SCAN_BLOCK + (b0 + 2) * SUBLANES)
            xb_ref[rows2, re_cols] = jnp.concatenate(halves_r, axis=0).astype(BF16)
            xb_ref[rows2, im_cols] = jnp.concatenate(halves_i, axis=0).astype(BF16)
    xcar_ref[0:1, :] = xc_r[0:1, :]
    xcar_ref[1:2, :] = xc_i[0:1, :]

    y_s = jnp.dot(xb_ref[...], s5_c_ref[...], preferred_element_type=F32)
    y_s = jax.nn.gelu(y_s + s5_d_ref[...] * us_p)
    glu = jnp.dot(y_s.astype(BF16), wglu_ref[...], preferred_element_type=F32) + bglu_ref[...]
    y_ssm = y_s * jax.nn.sigmoid(glu)
    for k in range(n_blk):
        for b in range(SUBLANES):
            for i in range(ssm_slabs):
                ycat_ref[lru_slabs + i, scan_rows(k, b), :] = y_ssm[blk_rows(k, b), i * LANES:(i + 1) * LANES]

    retention_units(len(ret_units))
    for p in range(ret_pairs):
        rstate_ref[p] = ret_state[p]
    o_all = jnp.concatenate([jnp.concatenate(ret_out[p], axis=0) for p in range(ret_pairs)], axis=1)
    def head_mean(v):
        parts = []
        for c0 in range(0, d_ret, MXU_DIM):
            c1 = min(c0 + MXU_DIM, d_ret)
            parts.append(jnp.dot(v[:, c0:c1].astype(BF16), avg_ref[c0:c1, c0:c1], preferred_element_type=F32))
        return jnp.concatenate(parts, axis=1)

    mu = head_mean(o_all)
    dev = o_all - mu
    d2 = dev * dev
    var = head_mean(d2)
    o_n = dev * lax.rsqrt(var + LN_EPS) * gn_g_ref[...] + gn_b_ref[...]
    y_ret = _silu(z_cols(o_gret, d_ret)) * o_n

    y_lru = [ycat_ref[i] * jax.nn.gelu(z_cols(o_glru + i * LANES, LANES))
             for i in range(lru_slabs)]
    y_ssm_t = [ycat_ref[lru_slabs + i] for i in range(ssm_slabs)]
    ycat = jnp.concatenate([v.astype(BF16) for v in y_lru] + [y_ret.astype(BF16)]
                           + [v.astype(BF16) for v in y_ssm_t], axis=1)
    m = jnp.dot(ycat, w_out_ref[...], preferred_element_type=F32)
    y = alpha * x + gate * m
    n_ln_blocks = max(1, len(proj_chunks))
    ln_rows = tile // n_ln_blocks
    for r in range(n_ln_blocks):
        rows = slice(r * ln_rows, (r + 1) * ln_rows)
        o_ref[0, rows, :] = _layer_norm_rows(y[rows], lng_ref[...], lnb_ref[...])
        project(1)
    project(len(proj_chunks))


def _mixer_call(x, pos3, mod, prm, ln_g, ln_b, *, layer, alpha, tile):
    bsz, seq, d = x.shape
    depth = prm["lam"].shape[0]
    d_lru = prm["lam"].shape[2]
    d_ret = prm["gn_g"].shape[2]
    d_ssm = prm["s5_d"].shape[2]
    n_state = prm["s5_lr"].shape[2]
    n_in = prm["w_in"].shape[2]
    n_scan_slabs = (d_lru + d_ssm) // LANES
    tiles_per_seq = seq // tile
    n_tiles = bsz * tiles_per_seq
    mod_rows = mod.shape[0] // (depth * 3)
    names = ["w_in", "conv_w", "conv_b", "wg", "bg", "lam", "lg", "gn_g", "gn_b", "avg",
             "s5_lr", "s5_li", "s5_dt", "s5_bre", "s5_bim", "s5_c", "s5_d", "wglu", "bglu", "w_out"]
    consts = [prm[k] for k in names]
    scratch = [
        pltpu.VMEM((tile, n_in), F32),
        pltpu.VMEM((n_scan_slabs, tile, LANES), F32),
        pltpu.VMEM((tile, n_in), F32),
        pltpu.VMEM((n_scan_slabs, tile, LANES), F32),
        pltpu.VMEM((n_scan_slabs, tile, LANES), F32),
        pltpu.VMEM(((CONV_WIDTH - 1) * SUBLANES, d_lru), F32),
        pltpu.VMEM((1, d_lru), F32),
        pltpu.VMEM((tile, 2 * n_state), F32),
        pltpu.VMEM((tile, 2 * n_state), BF16),
        pltpu.VMEM((2, n_state), F32),
        pltpu.VMEM((_ZT_ROWS, SUBLANES, n_state), F32),
        pltpu.VMEM((d_ssm, 2 * n_state), BF16),
        pltpu.VMEM((d_ret // LANES, RET_CHUNK, 2 * RET_CHUNK), F32),
        pltpu.VMEM((RET_CHUNK, d_ret), F32),
        pltpu.VMEM((RET_CHUNK, d_ret), F32),
        pltpu.VMEM((1, d_ret), F32),
        pltpu.VMEM((d_ret // LANES, LANES, LANES), F32),
    ]

    def proj_tile(g):
        return jnp.minimum(g, n_tiles - 1)

    def mix_tile(g):
        return jnp.maximum(g - 1, 0)

    def x_index(t):
        return (t // tiles_per_seq, t % tiles_per_seq, 0)

    def mod_index(t):
        return ((layer * mod_rows + t // tiles_per_seq) * 3 + 1, 0, 0)

    return pl.pallas_call(
        functools.partial(_mixer_kernel, alpha=alpha, tile=tile, tiles_per_seq=tiles_per_seq,
                          n_tiles=n_tiles, d_lru=d_lru, d_ret=d_ret, d_ssm=d_ssm),
        grid=(n_tiles + 1,),
        in_specs=[
            pl.BlockSpec((1, tile, d), lambda g: x_index(proj_tile(g))),
            pl.BlockSpec((None, 3, d), lambda g: mod_index(proj_tile(g))),
            pl.BlockSpec((1, tile, d), lambda g: x_index(mix_tile(g))),
            pl.BlockSpec((1, 1, tile), lambda g: (mix_tile(g), 0, 0)),
            pl.BlockSpec((None, 3, d), lambda g: mod_index(mix_tile(g))),
        ] + [_layer_spec(a.shape, layer) for a in consts]
          + [_layer_spec(ln_g.shape, layer * 3 + 1), _layer_spec(ln_b.shape, layer * 3 + 1)],
        out_specs=pl.BlockSpec((1, tile, d), lambda g: x_index(mix_tile(g))),
        out_shape=jax.ShapeDtypeStruct(x.shape, F32),
        scratch_shapes=scratch,
        compiler_params=pltpu.CompilerParams(
            dimension_semantics=("arbitrary",),
            vmem_limit_bytes=VMEM_LIMIT_BYTES),
    )(x, mod, x, pos3, mod, *consts, ln_g, ln_b)


def _block_diag(blocks):
    depth, n, r, c = blocks.shape
    eye = jnp.eye(n, dtype=blocks.dtype)
    return (eye[None, :, None, :, None] * blocks[:, :, :, None, :]).reshape(depth, n * r, n * c)


def _gate_groups(a, b):
    n = a.shape[-1]
    parts = []
    for c0 in range(0, n, MXU_DIM):
        parts += [a[..., c0:c0 + MXU_DIM], b[..., c0:c0 + MXU_DIM]]
    return jnp.concatenate(parts, axis=-1)


def _mixer_params(mix_w_in, conv_w, conv_b, lru_wa, lru_ba, lru_wx, lru_bx, lru_lam,
                  ret_gn_g, ret_gn_b, ssm_lam_re, ssm_lam_im, ssm_log_step, ssm_b_re, ssm_b_im,
                  ssm_c_re, ssm_c_im, ssm_d, ssm_w_glu, ssm_b_glu, mix_w_out):
    depth, d_lru = lru_lam.shape
    d_ret = ret_gn_g.shape[1]
    d_ssm = ssm_d.shape[1]
    groups, n_per = ssm_lam_re.shape[1], ssm_lam_re.shape[2]
    n_state = groups * n_per
    head_dim = d_ret // RET_HEADS
    log_gamma = np.array([math.log1p(-2.0 ** (-5.0 - hh)) for hh in range(RET_HEADS)], np.float64)
    lg_lane = np.repeat(log_gamma, head_dim)[None, None, :].repeat(depth, axis=0)
    avg = np.kron(np.eye(RET_HEADS), np.full((head_dim, head_dim), 1.0 / head_dim))[None].repeat(depth, axis=0)
    c_re_bd = _block_diag(jnp.swapaxes(ssm_c_re, 2, 3))
    c_im_bd = _block_diag(jnp.swapaxes(ssm_c_im, 2, 3))
    return {
        "w_in": mix_w_in.astype(BF16),
        "conv_w": conv_w,
        "conv_b": conv_b.reshape(depth, 1, d_lru),
        "wg": _gate_groups(_block_diag(lru_wa), _block_diag(lru_wx)).astype(BF16),
        "bg": _gate_groups(lru_ba[:, None, :], lru_bx[:, None, :]),
        "lam": lru_lam.reshape(depth, 1, d_lru),
        "lg": jnp.asarray(lg_lane, F32),
        "gn_g": ret_gn_g.reshape(depth, 1, d_ret),
        "gn_b": ret_gn_b.reshape(depth, 1, d_ret),
        "avg": jnp.asarray(avg, BF16),
        "s5_lr": ssm_lam_re.reshape(depth, 1, n_state),
        "s5_li": ssm_lam_im.reshape(depth, 1, n_state),
        "s5_dt": jnp.repeat(ssm_log_step, n_per, axis=1).reshape(depth, 1, n_state),
        "s5_bre": _block_diag(jnp.swapaxes(ssm_b_re, 2, 3)),
        "s5_bim": _block_diag(jnp.swapaxes(ssm_b_im, 2, 3)),
        "s5_c": jnp.concatenate([c_re_bd, -c_im_bd], axis=1).astype(BF16),
        "s5_d": ssm_d.reshape(depth, 1, d_ssm),
        "wglu": ssm_w_glu.astype(BF16),
        "bglu": ssm_b_glu.reshape(depth, 1, d_ssm),
        "w_out": mix_w_out.astype(BF16),
    }


def _tiles(seq):
    tm = 512 if seq % 512 == 0 else seq
    tile = 512 if seq % 512 == 0 else seq
    return tm, tile


def kernel(x, c, positions, ada_w, ada_b, ln_g, ln_b, ffn1_w1, ffn1_w3, ffn1_w2, mix_w_in, conv_w, conv_b, lru_wa, lru_ba, lru_wx, lru_bx, lru_lam, ret_gn_g, ret_gn_b, ssm_lam_re, ssm_lam_im, ssm_log_step, ssm_b_re, ssm_b_im, ssm_c_re, ssm_c_im, ssm_d, ssm_w_glu, ssm_b_glu, mix_w_out, ffn2_w1, ffn2_w3, ffn2_w2):
    bsz, seq, d = x.shape
    depth = ada_w.shape[0]
    alpha = (2.0 * depth) ** 0.25
    tm, tile = _tiles(seq)
    f_chunk = 256

    rows = -(-bsz // SUBLANES) * SUBLANES
    c_pad = jnp.pad(c, ((0, rows - bsz), (0, 0)))
    mod = _ada_call(c_pad, ada_w, ada_b).reshape(depth * rows * 3, 3, d)
    pos3 = positions.reshape(bsz * (seq // tile), 1, tile)
    lng = ln_g.reshape(depth * 3, 1, d)
    lnb = ln_b.reshape(depth * 3, 1, d)
    ffn1 = (ffn1_w1, ffn1_w3, ffn1_w2)
    ffn2 = (ffn2_w1, ffn2_w3, ffn2_w2)
    prm = _mixer_params(mix_w_in, conv_w, conv_b, lru_wa, lru_ba, lru_wx, lru_bx, lru_lam,
                        ret_gn_g, ret_gn_b, ssm_lam_re, ssm_lam_im, ssm_log_step, ssm_b_re, ssm_b_im,
                        ssm_c_re, ssm_c_im, ssm_d, ssm_w_glu, ssm_b_glu, mix_w_out)

    for l in range(depth):
        x = _ffn_call(x, mod, *ffn1, lng, lnb, layer=l, sub=0, alpha=alpha, tm=tm, f_chunk=f_chunk)
        x = _mixer_call(x, pos3, mod, prm, lng, lnb, layer=l, alpha=alpha, tile=tile)
        x = _ffn_call(x, mod, *ffn2, lng, lnb, layer=l, sub=2, alpha=alpha, tm=tm, f_chunk=f_chunk)
    return x
```

```python
import functools
import math

import numpy as np
import jax
import jax.numpy as jnp
from jax import lax
from jax.experimental import pallas as pl
from jax.experimental.pallas import tpu as pltpu

F32 = jnp.float32
BF16 = jnp.bfloat16

LRU_HEADS = 6
CONV_WIDTH = 4
LRU_C = 8.0
RET_HEADS = 6
RET_CHUNK = 128
ROPE_BASE = 10000.0
SSM_GROUP = 16
SSM_STATE = 64
N_MOD = 9
MACARON_HALF = 0.5
LN_EPS = 1e-5

LANES = 128
SUBLANES = 8
VMEM_LIMIT_BYTES = 56 * 1024 * 1024
SCAN_BLOCK = SUBLANES * SUBLANES
MXU_DIM = 256
PROJ_CHUNK = MXU_DIM


def _silu(v):
    return v * jax.nn.sigmoid(v)


def _layer_norm_rows(v, g, b):
    mu = jnp.mean(v, axis=-1, keepdims=True)
    d = v - mu
    var = jnp.mean(d * d, axis=-1, keepdims=True)
    return d * lax.rsqrt(var + LN_EPS) * g + b


def _layer_spec(shape, layer):
    tail = (0,) * (len(shape) - 1)
    return pl.BlockSpec((None,) + tuple(shape[1:]), lambda *_: (layer,) + tail,
                        pipeline_mode=pl.Buffered(1))


def _ada_kernel(c_ref, w_ref, b_ref, o_ref):
    cond = _silu(c_ref[...]).astype(BF16)
    o_ref[0] = jnp.dot(cond, w_ref[0].astype(BF16), preferred_element_type=F32) + b_ref[0]


def _ada_call(c_pad, ada_w, ada_b):
    depth, d, n = ada_w.shape
    rows = c_pad.shape[0]
    tn = n // N_MOD
    return pl.pallas_call(
        _ada_kernel,
        grid=(depth, n // tn),
        in_specs=[
            pl.BlockSpec((rows, d), lambda l, j: (0, 0)),
            pl.BlockSpec((1, d, tn), lambda l, j: (l, 0, j)),
            pl.BlockSpec((1, 1, tn), lambda l, j: (l, 0, j)),
        ],
        out_specs=pl.BlockSpec((1, rows, tn), lambda l, j: (l, 0, j)),
        out_shape=jax.ShapeDtypeStruct((depth, rows, n), F32),
        compiler_params=pltpu.CompilerParams(
            dimension_semantics=("arbitrary", "arbitrary"),
            vmem_limit_bytes=VMEM_LIMIT_BYTES),
    )(c_pad, ada_w, ada_b.reshape(depth, 1, n))


def _ffn_kernel(x_ref, mod_ref, w1_ref, w3_ref, w2_ref, lng_ref, lnb_ref, o_ref, y_ref,
                *, alpha, f_chunk, n_tiles):
    tm = x_ref.shape[1]
    d_ff = w1_ref.shape[1]
    f_starts = list(range(0, d_ff, f_chunk))
    ln_rows = tm // SUBLANES

    def step(do_matmul, do_norm):
        ln_blocks = list(range(0, tm, ln_rows)) if do_norm else []
        anchor = {"from_mm": None, "from_ln": None}

        def norm_rows():
            if ln_blocks:
                r0 = ln_blocks.pop(0)
                y_blk = y_ref[r0:r0 + ln_rows, :]
                if anchor["from_mm"] is not None:
                    y_blk = y_blk + anchor["from_mm"]
                out = _layer_norm_rows(y_blk, lng_ref[...], lnb_ref[...])
                o_ref[0, r0:r0 + ln_rows, :] = out
                anchor["from_ln"] = out[0:1, :] * 0.0

        if do_matmul:
            x = x_ref[0]
            shift = mod_ref[0:1, :]
            scale = mod_ref[1:2, :]
            gate = mod_ref[2:3, :]
            h = (x * (1.0 + scale) + shift).astype(BF16)
            g_chunks = []
            for f0 in f_starts:
                h_c = h
                if anchor["from_ln"] is not None:
                    top = h[0:2 * SUBLANES, :] + anchor["from_ln"].astype(BF16)
                    h_c = jnp.concatenate([top, h[2 * SUBLANES:, :]], axis=0)
                a = jnp.dot(h_c, w1_ref[:, f0:f0 + f_chunk].astype(BF16), preferred_element_type=F32)
                b = jnp.dot(h_c, w3_ref[:, f0:f0 + f_chunk].astype(BF16), preferred_element_type=F32)
                g_chunks.append((_silu(a) * b).astype(BF16))
                norm_rows()
                if do_norm:
                    anchor["from_mm"] = a[0:1, 0:1] * 0.0
            acc = jnp.dot(jnp.concatenate(g_chunks, axis=1), w2_ref[...].astype(BF16),
                          preferred_element_type=F32)
        while ln_blocks:
            norm_rows()
        if do_matmul:
            y_ref[...] = alpha * x + (MACARON_HALF * gate) * acc

    g_idx = pl.program_id(0)
    pl.when(g_idx == 0)(functools.partial(step, True, False))
    pl.when(jnp.logical_and(g_idx > 0, g_idx < n_tiles))(functools.partial(step, True, True))
    pl.when(g_idx == n_tiles)(functools.partial(step, False, True))


def _ffn_call(x, mod, w1, w3, w2, ln_g, ln_b, *, layer, sub, alpha, tm, f_chunk):
    bsz, seq, d = x.shape
    mod_rows = mod.shape[0] // (w1.shape[0] * 3)
    tiles_per_seq = seq // tm
    n_tiles = bsz * tiles_per_seq

    def mm_tile(g):
        return jnp.minimum(g, n_tiles - 1)

    def ln_tile(g):
        return jnp.maximum(g - 1, 0)

    def x_index(t):
        return (t // tiles_per_seq, t % tiles_per_seq, 0)

    return pl.pallas_call(
        functools.partial(_ffn_kernel, alpha=alpha, f_chunk=f_chunk, n_tiles=n_tiles),
        grid=(n_tiles + 1,),
        in_specs=[
            pl.BlockSpec((1, tm, d), lambda g: x_index(mm_tile(g))),
            pl.BlockSpec((None, 3, d), lambda g: ((layer * mod_rows + mm_tile(g) // tiles_per_seq) * 3 + sub, 0, 0)),
            _layer_spec(w1.shape, layer),
            _layer_spec(w3.shape, layer),
            _layer_spec(w2.shape, layer),
            _layer_spec(ln_g.shape, layer * 3 + sub),
            _layer_spec(ln_b.shape, layer * 3 + sub),
        ],
        out_specs=pl.BlockSpec((1, tm, d), lambda g: x_index(ln_tile(g))),
        out_shape=jax.ShapeDtypeStruct(x.shape, F32),
        scratch_shapes=[pltpu.VMEM((tm, d), F32)],
        compiler_params=pltpu.CompilerParams(
            dimension_semantics=("arbitrary",),
            vmem_limit_bytes=VMEM_LIMIT_BYTES),
    )(x, mod, w1, w3, w2, ln_g, ln_b)


_ZT_STEP_RE = 0
_ZT_STEP_IM = SUBLANES
_ZT_HS_RE = 2 * SUBLANES
_ZT_HS_IM = 2 * SUBLANES + 3
_ZT_ROWS = 2 * SUBLANES + 6
_HS_SHIFTS = (1, 2, 4)


def _swap_halves(v, lane_in_head, half):
    n = v.shape[-1]
    fwd = pltpu.roll(v, half, 1)
    bwd = pltpu.roll(v, n - half, 1)
    return jnp.where(lane_in_head < half, bwd, fwd)


def _row_bcast(v, row):
    return jnp.broadcast_to(v[row:row + 1, :], v.shape)


_N_MIXER_IO = 28


def _mixer_kernel(*refs, **static):
    io, (z_a, zs_a, z_b, zs_b), rest = refs[:_N_MIXER_IO], refs[_N_MIXER_IO:_N_MIXER_IO + 4], refs[_N_MIXER_IO + 4:]
    n_tiles = static.pop("n_tiles")
    g_idx = pl.program_id(0)
    inner = jnp.logical_and(g_idx > 0, g_idx < n_tiles)
    even = (z_a, zs_a, z_b, zs_b)
    odd = (z_b, zs_b, z_a, zs_a)

    def run(bufs, **phases):
        return functools.partial(_mixer_step, *io, *bufs, *rest, **static, **phases)

    pl.when(g_idx == 0)(run(even, do_proj=True, do_mix=False))
    pl.when(jnp.logical_and(inner, g_idx % 2 == 0))(run(even, do_proj=True, do_mix=True))
    pl.when(jnp.logical_and(inner, g_idx % 2 == 1))(run(odd, do_proj=True, do_mix=True))
    pl.when(g_idx == n_tiles)(run(even if n_tiles % 2 == 0 else odd, do_proj=False, do_mix=True))


def _mixer_step(
        xp_ref, modp_ref,
        x_ref, pos_ref, mod_ref, w_in_ref, conv_w_ref, conv_b_ref, wg_ref, bg_ref, lam_ref,
        lg_ref, gn_g_ref, gn_b_ref, avg_ref,
        s5_lr_ref, s5_li_ref, s5_dt_ref, s5_bre_ref, s5_bim_ref, s5_c_ref, s5_d_ref,
        wglu_ref, bglu_ref, w_out_ref, lng_ref, lnb_ref,
        o_ref,
        zp_ref, zsp_ref, zc_ref, zsc_ref,
        ycat_ref, tail_ref, hcar_ref, bu_ref, xb_ref, xcar_ref, ztab_ref, bbar_ref,
        dmask_ref, qdec_ref, kdec_ref, cdec_ref, rstate_ref,
        *, alpha, tile, tiles_per_seq, d_lru, d_ret, d_ssm, do_proj, do_mix):
    g_idx = pl.program_id(0)
    n_state = bu_ref.shape[1] // 2
    n_blk = tile // SCAN_BLOCK
    lru_slabs = d_lru // LANES
    ssm_slabs = d_ssm // LANES
    ret_pairs = d_ret // LANES
    head_dim = d_ret // RET_HEADS
    half = head_dim // 2

    def _init_tables():
        lr = s5_lr_ref[...]
        li = s5_li_ref[...]
        dt = jnp.exp(s5_dt_ref[...])
        sub = lax.broadcasted_iota(jnp.int32, (SUBLANES, n_state), 0)

        def z_power(n):
            mag = jnp.exp(n * (lr * dt))
            ang = n * (li * dt)
            return mag * jnp.cos(ang), mag * jnp.sin(ang)

        st_r, st_i = z_power((sub + 1).astype(F32))
        ch_r, ch_i = z_power((SUBLANES * (sub + 1)).astype(F32))
        for b in range(SUBLANES):
            ztab_ref[_ZT_STEP_RE + b] = _row_bcast(st_r, b)
            ztab_ref[_ZT_STEP_IM + b] = _row_bcast(st_i, b)
        for n, d in enumerate(_HS_SHIFTS):
            ztab_ref[_ZT_HS_RE + n] = jnp.where(sub >= d, _row_bcast(ch_r, d - 1), 0.0)
            ztab_ref[_ZT_HS_IM + n] = jnp.where(sub >= d, _row_bcast(ch_i, d - 1), 0.0)
        zr = st_r[0:1, :]
        zi = st_i[0:1, :]
        den = lr * lr + li * li
        er = ((zr - 1.0) * lr + zi * li) / den
        ei = (zi * lr - (zr - 1.0) * li) / den
        bre = s5_bre_ref[...]
        bim = s5_bim_ref[...]
        bbar_ref[:, 0:n_state] = (er * bre - ei * bim).astype(BF16)
        bbar_ref[:, n_state:2 * n_state] = (er * bim + ei * bre).astype(BF16)
        tq = lax.broadcasted_iota(jnp.int32, (RET_CHUNK, 2 * RET_CHUNK), 0)
        tk = lax.broadcasted_iota(jnp.int32, (RET_CHUNK, 2 * RET_CHUNK), 1) % RET_CHUNK
        diff = (tq - tk).astype(F32)
        row = lax.broadcasted_iota(jnp.int32, (RET_CHUNK, d_ret), 0).astype(F32)
        lg_row = lg_ref[...]
        qdec_ref[...] = jnp.exp(lg_row * (row + 1.0))
        kdec_ref[...] = jnp.exp(lg_row * (RET_CHUNK - 1.0 - row))
        cdec_ref[...] = jnp.exp(lg_row * float(RET_CHUNK))
        for p in range(ret_pairs):
            lg_lo = lg_ref[0:1, p * LANES:p * LANES + 1]
            lg_hi = lg_ref[0:1, p * LANES + head_dim:p * LANES + head_dim + 1]
            col = lax.broadcasted_iota(jnp.int32, (RET_CHUNK, 2 * RET_CHUNK), 1)
            lg_pair = jnp.where(col < RET_CHUNK, lg_lo, lg_hi)
            dmask_ref[p] = jnp.where(diff >= 0.0, jnp.exp(lg_pair * jnp.maximum(diff, 0.0)), 0.0)

    if not do_mix:
        _init_tables()

    if do_mix:
        @pl.when((g_idx - 1) % tiles_per_seq == 0)
        def _reset_state():
            tail_ref[...] = jnp.zeros(tail_ref.shape, F32)
            hcar_ref[...] = jnp.zeros(hcar_ref.shape, F32)
            xcar_ref[...] = jnp.zeros(xcar_ref.shape, F32)
            rstate_ref[...] = jnp.zeros(rstate_ref.shape, F32)

    o_lru, o_glru = 0, d_lru
    o_q, o_k, o_v, o_gret = 2 * d_lru, 2 * d_lru + d_ret, 2 * d_lru + 2 * d_ret, 2 * d_lru + 3 * d_ret
    o_ssm = 2 * d_lru + 4 * d_ret

    if do_proj:
        xp = xp_ref[0]
        hp = (xp * (1.0 + modp_ref[1:2, :]) + modp_ref[0:1, :]).astype(BF16)
    scan_cols = ([o_lru + i * LANES for i in range(lru_slabs)]
                 + [o_ssm + i * LANES for i in range(ssm_slabs)])
    proj_chunks = list(range(0, w_in_ref.shape[1], PROJ_CHUNK)) if do_proj else []

    def project(n_chunks):
        for _ in range(min(n_chunks, len(proj_chunks))):
            c0 = proj_chunks.pop(0)
            zchunk = jnp.dot(hp, w_in_ref[:, c0:c0 + PROJ_CHUNK], preferred_element_type=F32)
            zp_ref[:, c0:c0 + PROJ_CHUNK] = zchunk
            for s_idx, sc in enumerate(scan_cols):
                if c0 <= sc < c0 + PROJ_CHUNK:
                    zsp_ref[s_idx] = zchunk[:, sc - c0:sc - c0 + LANES]

    if not do_mix:
        project(len(proj_chunks))
        return

    x = x_ref[0]
    gate = mod_ref[2:3, :]

    def z_cols(c0, width, r0=0, n_rows=tile):
        return zc_ref[r0:r0 + n_rows, c0:c0 + width]

    def scan_rows(k, b):
        return pl.ds(k * SCAN_BLOCK + b, SUBLANES, stride=SUBLANES)

    def blk_rows(k, b):
        return slice(k * SCAN_BLOCK + b * SUBLANES, k * SCAN_BLOCK + (b + 1) * SUBLANES)

    u_rows = [[jnp.concatenate([zsc_ref[i, scan_rows(k, b), :] for i in range(lru_slabs)], axis=1)
               for b in range(SUBLANES)] for k in range(n_blk)]
    us_rows = [[jnp.concatenate([zsc_ref[lru_slabs + i, scan_rows(k, b), :] for i in range(ssm_slabs)], axis=1)
                for b in range(SUBLANES)] for k in range(n_blk)]

    sub_l = lax.broadcasted_iota(jnp.int32, (SUBLANES, d_lru), 0)
    sub_s = lax.broadcasted_iota(jnp.int32, (SUBLANES, n_state), 0)

    pos_row = pos_ref[0].astype(F32)
    freq = lax.broadcasted_iota(jnp.int32, (half, 1), 0).astype(F32)
    inv = jnp.exp(freq * (-math.log(ROPE_BASE) / half))
    ang_t = inv * pos_row
    cos_t = jnp.cos(ang_t)
    sin_t = jnp.sin(ang_t)
    reps = LANES // head_dim
    cos_tab = jnp.concatenate([cos_t, cos_t] * reps, axis=0).T
    sin_tab = jnp.concatenate([-sin_t, sin_t] * reps, axis=0).T
    lane_in_head = lax.broadcasted_iota(jnp.int32, (RET_CHUNK, LANES), 1) % head_dim
    lane = lax.broadcasted_iota(jnp.int32, (RET_CHUNK, LANES), 1)
    lo_mask = lane < head_dim
    blk_r = lax.broadcasted_iota(jnp.int32, (LANES, LANES), 0) // head_dim
    blk_c = lax.broadcasted_iota(jnp.int32, (LANES, LANES), 1) // head_dim
    same_head = blk_r == blk_c
    inv_sqrt_dh = head_dim ** -0.5
    ret_state = [rstate_ref[p] for p in range(ret_pairs)]
    ret_out = [[] for _ in range(ret_pairs)]
    ret_units = [(ci_, p) for ci_ in range(tile // RET_CHUNK) for p in range(ret_pairs)]

    def retention_units(n_units):
        for _ in range(min(n_units, len(ret_units))):
            ci_, p = ret_units.pop(0)
            rows = slice(ci_ * RET_CHUNK, (ci_ + 1) * RET_CHUNK)
            cols = slice(p * LANES, (p + 1) * LANES)
            cs = cos_tab[rows]
            sn = sin_tab[rows]
            q_c = z_cols(o_q + p * LANES, LANES, ci_ * RET_CHUNK, RET_CHUNK)
            k_c = z_cols(o_k + p * LANES, LANES, ci_ * RET_CHUNK, RET_CHUNK)
            v_c = z_cols(o_v + p * LANES, LANES, ci_ * RET_CHUNK, RET_CHUNK)
            qr = q_c * cs + _swap_halves(q_c, lane_in_head, half) * sn
            kr = (k_c * cs + _swap_halves(k_c, lane_in_head, half) * sn) * inv_sqrt_dh
            k_bd = jnp.concatenate([jnp.where(lo_mask, kr, 0.0), jnp.where(lo_mask, 0.0, kr)], axis=0)
            v_bd = jnp.concatenate([jnp.where(lo_mask, v_c, 0.0), jnp.where(lo_mask, 0.0, v_c)], axis=0)
            scores = lax.dot_general(qr.astype(BF16), k_bd.astype(BF16), (((1,), (1,)), ((), ())),
                                     preferred_element_type=F32)
            scores = scores * dmask_ref[p]
            o_c = jnp.dot(scores.astype(BF16), v_bd.astype(BF16), preferred_element_type=F32)
            o_c = o_c + jnp.dot((qr * qdec_ref[:, cols]).astype(BF16), ret_state[p].astype(BF16),
                                preferred_element_type=F32)
            kv = jnp.dot((kr * kdec_ref[:, cols]).T.astype(BF16), v_c.astype(BF16), preferred_element_type=F32)
            ret_state[p] = cdec_ref[:, cols] * ret_state[p] + jnp.where(same_head, kv, 0.0)
            ret_out[p].append(o_c)

    n_tail = CONV_WIDTH - 1
    uc_rows = []
    for k in range(n_blk):
        early = {}
        for dd in range(1, n_tail + 1):
            cur = u_rows[k][SUBLANES - dd]
            if k == 0:
                prev = tail_ref[(n_tail - dd) * SUBLANES:(n_tail - dd + 1) * SUBLANES, :]
            else:
                prev = u_rows[k - 1][SUBLANES - dd]
            early[-dd] = pltpu.roll(jnp.where(sub_l == SUBLANES - 1, prev, cur), 1, 0)
        for b in range(SUBLANES):
            acc = conv_b_ref[...]
            for kk in range(CONV_WIDTH):
                src = b - n_tail + kk
                acc = acc + (u_rows[k][src] if src >= 0 else early[src]) * conv_w_ref[kk:kk + 1, :]
            uc_rows.append(acc)
    for dd in range(1, n_tail + 1):
        tail_ref[(n_tail - dd) * SUBLANES:(n_tail - dd + 1) * SUBLANES, :] = u_rows[n_blk - 1][SUBLANES - dd]
    project(1)
    uc = jnp.concatenate(uc_rows, axis=0)
    uc_bf = uc.astype(BF16)
    r_parts, i_parts = [], []
    for c0 in range(0, d_lru, MXU_DIM):
        wid = min(MXU_DIM, d_lru - c0)
        gz = (jnp.dot(uc_bf[:, c0:c0 + wid], wg_ref[c0:c0 + wid, 2 * c0:2 * (c0 + wid)],
                      preferred_element_type=F32) + bg_ref[:, 2 * c0:2 * (c0 + wid)])
        r_parts.append(gz[:, 0:wid])
        i_parts.append(gz[:, wid:2 * wid])
    gz_r = jnp.concatenate(r_parts, axis=1)
    gz_i = jnp.concatenate(i_parts, axis=1)
    project(1)
    lam = lam_ref[...]
    softplus_neg_lam = jnp.maximum(-lam, 0.0) + jnp.log(1.0 + jnp.exp(-jnp.abs(lam)))
    log_a_scale = -LRU_C * softplus_neg_lam
    project(1)

    h_car = jnp.broadcast_to(hcar_ref[...], (SUBLANES, d_lru))
    for k in range(n_blk):
        if k % 4 == 0:
            project(1)
        blk = slice(k * SCAN_BLOCK, (k + 1) * SCAN_BLOCK)
        a_blk = jnp.exp(log_a_scale * jax.nn.sigmoid(gz_r[blk]))
        b_blk = jnp.sqrt(1.0 - a_blk * a_blk) * (jax.nn.sigmoid(gz_i[blk]) * uc[blk])
        h_prev = jnp.where(sub_l == 0, h_car, 0.0)
        h_loc, a_cum = [], []
        for b in range(SUBLANES):
            a_b = a_blk[b * SUBLANES:(b + 1) * SUBLANES]
            h_prev = a_b * h_prev + b_blk[b * SUBLANES:(b + 1) * SUBLANES]
            h_loc.append(h_prev)
            a_cum.append(a_b if b == 0 else a_b * a_cum[-1])
        pa, pb = a_cum[-1], h_loc[-1]
        for d in _HS_SHIFTS:
            live = sub_l >= d
            pb = jnp.where(live, pa * pltpu.roll(pb, d, 0) + pb, pb)
            pa = jnp.where(live, pa * pltpu.roll(pa, d, 0), pa)
        h_in = jnp.where(sub_l == 0, 0.0, pltpu.roll(pb, 1, 0))
        h_car = _row_bcast(pb, SUBLANES - 1)
        for b in range(SUBLANES):
            h_t = h_loc[b] + a_cum[b] * h_in
            for i in range(lru_slabs):
                ycat_ref[i, scan_rows(k, b), :] = h_t[:, i * LANES:(i + 1) * LANES]
    hcar_ref[...] = h_car[0:1, :]

    us_p = jnp.concatenate([us_rows[k][b] for k in range(n_blk) for b in range(SUBLANES)], axis=0)
    bu_ref[...] = jnp.dot(us_p.astype(BF16), bbar_ref[...], preferred_element_type=F32)
    re_cols = slice(0, n_state)
    im_cols = slice(n_state, 2 * n_state)
    zr = ztab_ref[_ZT_STEP_RE]
    zi = ztab_ref[_ZT_STEP_IM]
    xc_r = jnp.broadcast_to(xcar_ref[0:1, :], (SUBLANES, n_state))
    xc_i = jnp.broadcast_to(xcar_ref[1:2, :], (SUBLANES, n_state))
    for k in range(n_blk):
        if k % 3 == 0:
            project(1)
        retention_units(2 - k % 2)
        xr = jnp.where(sub_s == 0, xc_r, 0.0)
        xi = jnp.where(sub_s == 0, xc_i, 0.0)
        for b in range(SUBLANES):
            xr, xi = (zr * xr - zi * xi + bu_ref[blk_rows(k, b), re_cols],
                      zr * xi + zi * xr + bu_ref[blk_rows(k, b), im_cols])
            bu_ref[blk_rows(k, b), re_cols] = xr
            bu_ref[blk_rows(k, b), im_cols] = xi
        pr, pi = xr, xi
        for n, d in enumerate(_HS_SHIFTS):
            hr, hi = ztab_ref[_ZT_HS_RE + n], ztab_ref[_ZT_HS_IM + n]
            sr, si = pltpu.roll(pr, d, 0), pltpu.roll(pi, d, 0)
            pr, pi = pr + hr * sr - hi * si, pi + hr * si + hi * sr
        xin_r = jnp.where(sub_s == 0, 0.0, pltpu.roll(pr, 1, 0))
        xin_i = jnp.where(sub_s == 0, 0.0, pltpu.roll(pi, 1, 0))
        xc_r = _row_bcast(pr, SUBLANES - 1)
        xc_i = _row_bcast(pi, SUBLANES - 1)
        for b0 in range(0, SUBLANES, 2):
            halves_r, halves_i = [], []
            for b in (b0, b0 + 1):
                sr, si = ztab_ref[_ZT_STEP_RE + b], ztab_ref[_ZT_STEP_IM + b]
                halves_r.append(bu_ref[blk_rows(k, b), re_cols] + sr * xin_r - si * xin_i)
                halves_i.append(bu_ref[blk_rows(k, b), im_cols] + sr * xin_i + si * xin_r)
            rows2 = slice(k * SCAN_BLOCK + b0 * SUBLANES, k * SCAN_BLOCK + (b0 + 2) * SUBLANES)
            xb_ref[rows2, re_cols] = jnp.concatenate(halves_r, axis=0).astype(BF16)
            xb_ref[rows2, im_cols] = jnp.concatenate(halves_i, axis=0).astype(BF16)
    xcar_ref[0:1, :] = xc_r[0:1, :]
    xcar_ref[1:2, :] = xc_i[0:1, :]

    y_s = jnp.dot(xb_ref[...], s5_c_ref[...], preferred_element_type=F32)
    y_s = jax.nn.gelu(y_s + s5_d_ref[...] * us_p)
    glu = jnp.dot(y_s.astype(BF16), wglu_ref[...], preferred_element_type=F32) + bglu_ref[...]
    y_ssm = y_s * jax.nn.sigmoid(glu)
    for k in range(n_blk):
        for b in range(SUBLANES):
            for i in range(ssm_slabs):
                ycat_ref[lru_slabs + i, scan_rows(k, b), :] = y_ssm[blk_rows(k, b), i * LANES:(i + 1) * LANES]

    retention_units(len(ret_units))
    for p in range(ret_pairs):
        rstate_ref[p] = ret_state[p]
    o_all = jnp.concatenate([jnp.concatenate(ret_out[p], axis=0) for p in range(ret_pairs)], axis=1)
    def head_mean(v):
        parts = []
        for c0 in range(0, d_ret, MXU_DIM):
            c1 = min(c0 + MXU_DIM, d_ret)
            parts.append(jnp.dot(v[:, c0:c1].astype(BF16), avg_ref[c0:c1, c0:c1], preferred_element_type=F32))
        return jnp.concatenate(parts, axis=1)

    mu = head_mean(o_all)
    dev = o_all - mu
    d2 = dev * dev
    var = head_mean(d2)
    o_n = dev * lax.rsqrt(var + LN_EPS) * gn_g_ref[...] + gn_b_ref[...]
    y_ret = _silu(z_cols(o_gret, d_ret)) * o_n

    y_lru = [ycat_ref[i] * jax.nn.gelu(z_cols(o_glru + i * LANES, LANES))
             for i in range(lru_slabs)]
    y_ssm_t = [ycat_ref[lru_slabs + i] for i in range(ssm_slabs)]
    ycat = jnp.concatenate([v.astype(BF16) for v in y_lru] + [y_ret.astype(BF16)]
                           + [v.astype(BF16) for v in y_ssm_t], axis=1)
    m = jnp.dot(ycat, w_out_ref[...], preferred_element_type=F32)
    y = alpha * x + gate * m
    n_ln_blocks = max(1, len(proj_chunks))
    ln_rows = tile // n_ln_blocks
    for r in range(n_ln_blocks):
        rows = slice(r * ln_rows, (r + 1) * ln_rows)
        o_ref[0, rows, :] = _layer_norm_rows(y[rows], lng_ref[...], lnb_ref[...])
        project(1)
    project(len(proj_chunks))


def _mixer_call(x, pos3, mod, prm, ln_g, ln_b, *, layer, alpha, tile):
    bsz, seq, d = x.shape
    depth = prm["lam"].shape[0]
    d_lru = prm["lam"].shape[2]
    d_ret = prm["gn_g"].shape[2]
    d_ssm = prm["s5_d"].shape[2]
    n_state = prm["s5_lr"].shape[2]
    n_in = prm["w_in"].shape[2]
    n_scan_slabs = (d_lru + d_ssm) // LANES
    tiles_per_seq = seq // tile
    n_tiles = bsz * tiles_per_seq
    mod_rows = mod.shape[0] // (depth * 3)
    names = ["w_in", "conv_w", "conv_b", "wg", "bg", "lam", "lg", "gn_g", "gn_b", "avg",
             "s5_lr", "s5_li", "s5_dt", "s5_bre", "s5_bim", "s5_c", "s5_d", "wglu", "bglu", "w_out"]
    consts = [prm[k] for k in names]
    scratch = [
        pltpu.VMEM((tile, n_in), F32),
        pltpu.VMEM((n_scan_slabs, tile, LANES), F32),
        pltpu.VMEM((tile, n_in), F32),
        pltpu.VMEM((n_scan_slabs, tile, LANES), F32),
        pltpu.VMEM((n_scan_slabs, tile, LANES), F32),
        pltpu.VMEM(((CONV_WIDTH - 1) * SUBLANES, d_lru), F32),
        pltpu.VMEM((1, d_lru), F32),
        pltpu.VMEM((tile, 2 * n_state), F32),
        pltpu.VMEM((tile, 2 * n_state), BF16),
        pltpu.VMEM((2, n_state), F32),
        pltpu.VMEM((_ZT_ROWS, SUBLANES, n_state), F32),
        pltpu.VMEM((d_ssm, 2 * n_state), BF16),
        pltpu.VMEM((d_ret // LANES, RET_CHUNK, 2 * RET_CHUNK), F32),
        pltpu.VMEM((RET_CHUNK, d_ret), F32),
        pltpu.VMEM((RET_CHUNK, d_ret), F32),
        pltpu.VMEM((1, d_ret), F32),
        pltpu.VMEM((d_ret // LANES, LANES, LANES), F32),
    ]

    def proj_tile(g):
        return jnp.minimum(g, n_tiles - 1)

    def mix_tile(g):
        return jnp.maximum(g - 1, 0)

    def x_index(t):
        return (t // tiles_per_seq, t % tiles_per_seq, 0)

    def mod_index(t):
        return ((layer * mod_rows + t // tiles_per_seq) * 3 + 1, 0, 0)

    return pl.pallas_call(
        functools.partial(_mixer_kernel, alpha=alpha, tile=tile, tiles_per_seq=tiles_per_seq,
                          n_tiles=n_tiles, d_lru=d_lru, d_ret=d_ret, d_ssm=d_ssm),
        grid=(n_tiles + 1,),
        in_specs=[
            pl.BlockSpec((1, tile, d), lambda g: x_index(proj_tile(g))),
            pl.BlockSpec((None, 3, d), lambda g: mod_index(proj_tile(g))),
            pl.BlockSpec((1, tile, d), lambda g: x_index(mix_tile(g))),
            pl.BlockSpec((1, 1, tile), lambda g: (mix_tile(g), 0, 0)),
            pl.BlockSpec((None, 3, d), lambda g: mod_index(mix_tile(g))),
        ] + [_layer_spec(a.shape, layer) for a in consts]
          + [_layer_spec(ln_g.shape, layer * 3 + 1), _layer_spec(ln_b.shape, layer * 3 + 1)],
        out_specs=pl.BlockSpec((1, tile, d), lambda g: x_index(mix_tile(g))),
        out_shape=jax.ShapeDtypeStruct(x.shape, F32),
        scratch_shapes=scratch,
        compiler_params=pltpu.CompilerParams(
            dimension_semantics=("arbitrary",),
            vmem_limit_bytes=VMEM_LIMIT_BYTES),
    )(x, mod, x, pos3, mod, *consts, ln_g, ln_b)


def _block_diag(blocks):
    depth, n, r, c = blocks.shape
    eye = jnp.eye(n, dtype=blocks.dtype)
    return (eye[None, :, None, :, None] * blocks[:, :, :, None, :]).reshape(depth, n * r, n * c)


def _gate_groups(a, b):
    n = a.shape[-1]
    parts = []
    for c0 in range(0, n, MXU_DIM):
        parts += [a[..., c0:c0 + MXU_DIM], b[..., c0:c0 + MXU_DIM]]
    return jnp.concatenate(parts, axis=-1)


def _mixer_params(mix_w_in, conv_w, conv_b, lru_wa, lru_ba, lru_wx, lru_bx, lru_lam,
                  ret_gn_g, ret_gn_b, ssm_lam_re, ssm_lam_im, ssm_log_step, ssm_b_re, ssm_b_im,
                  ssm_c_re, ssm_c_im, ssm_d, ssm_w_glu, ssm_b_glu, mix_w_out):
    depth, d_lru = lru_lam.shape
    d_ret = ret_gn_g.shape[1]
    d_ssm = ssm_d.shape[1]
    groups, n_per = ssm_lam_re.shape[1], ssm_lam_re.shape[2]
    n_state = groups * n_per
    head_dim = d_ret // RET_HEADS
    log_gamma = np.array([math.log1p(-2.0 ** (-5.0 - hh)) for hh in range(RET_HEADS)], np.float64)
    lg_lane = np.repeat(log_gamma, head_dim)[None, None, :].repeat(depth, axis=0)
    avg = np.kron(np.eye(RET_HEADS), np.full((head_dim, head_dim), 1.0 / head_dim))[None].repeat(depth, axis=0)
    c_re_bd = _block_diag(jnp.swapaxes(ssm_c_re, 2, 3))
    c_im_bd = _block_diag(jnp.swapaxes(ssm_c_im, 2, 3))
    return {
        "w_in": mix_w_in.astype(BF16),
        "conv_w": conv_w,
        "conv_b": conv_b.reshape(depth, 1, d_lru),
        "wg": _gate_groups(_block_diag(lru_wa), _block_diag(lru_wx)).astype(BF16),
        "bg": _gate_groups(lru_ba[:, None, :], lru_bx[:, None, :]),
        "lam": lru_lam.reshape(depth, 1, d_lru),
        "lg": jnp.asarray(lg_lane, F32),
        "gn_g": ret_gn_g.reshape(depth, 1, d_ret),
        "gn_b": ret_gn_b.reshape(depth, 1, d_ret),
        "avg": jnp.asarray(avg, BF16),
        "s5_lr": ssm_lam_re.reshape(depth, 1, n_state),
        "s5_li": ssm_lam_im.reshape(depth, 1, n_state),
        "s5_dt": jnp.repeat(ssm_log_step, n_per, axis=1).reshape(depth, 1, n_state),
        "s5_bre": _block_diag(jnp.swapaxes(ssm_b_re, 2, 3)),
        "s5_bim": _block_diag(jnp.swapaxes(ssm_b_im, 2, 3)),
        "s5_c": jnp.concatenate([c_re_bd, -c_im_bd], axis=1).astype(BF16),
        "s5_d": ssm_d.reshape(depth, 1, d_ssm),
        "wglu": ssm_w_glu.astype(BF16),
        "bglu": ssm_b_glu.reshape(depth, 1, d_ssm),
        "w_out": mix_w_out.astype(BF16),
    }


def _tiles(seq):
    tm = 512 if seq % 512 == 0 else seq
    tile = 512 if seq % 512 == 0 else seq
    return tm, tile


def kernel(x, c, positions, ada_w, ada_b, ln_g, ln_b, ffn1_w1, ffn1_w3, ffn1_w2, mix_w_in, conv_w, conv_b, lru_wa, lru_ba, lru_wx, lru_bx, lru_lam, ret_gn_g, ret_gn_b, ssm_lam_re, ssm_lam_im, ssm_log_step, ssm_b_re, ssm_b_im, ssm_c_re, ssm_c_im, ssm_d, ssm_w_glu, ssm_b_glu, mix_w_out, ffn2_w1, ffn2_w3, ffn2_w2):
    bsz, seq, d = x.shape
    depth = ada_w.shape[0]
    alpha = (2.0 * depth) ** 0.25
    tm, tile = _tiles(seq)
    f_chunk = 256

    rows = -(-bsz // SUBLANES) * SUBLANES
    c_pad = jnp.pad(c, ((0, rows - bsz), (0, 0)))
    mod = _ada_call(c_pad, ada_w, ada_b).reshape(depth * rows * 3, 3, d)
    pos3 = positions.reshape(bsz * (seq // tile), 1, tile)
    lng = ln_g.reshape(depth * 3, 1, d)
    lnb = ln_b.reshape(depth * 3, 1, d)
    ffn1 = (ffn1_w1, ffn1_w3, ffn1_w2)
    ffn2 = (ffn2_w1, ffn2_w3, ffn2_w2)
    prm = _mixer_params(mix_w_in, conv_w, conv_b, lru_wa, lru_ba, lru_wx, lru_bx, lru_lam,
                        ret_gn_g, ret_gn_b, ssm_lam_re, ssm_lam_im, ssm_log_step, ssm_b_re, ssm_b_im,
                        ssm_c_re, ssm_c_im, ssm_d, ssm_w_glu, ssm_b_glu, mix_w_out)

    for l in range(depth):
        x = _ffn_call(x, mod, *ffn1, lng, lnb, layer=l, sub=0, alpha=alpha, tm=tm, f_chunk=f_chunk)
        x = _mixer_call(x, pos3, mod, prm, lng, lnb, layer=l, alpha=alpha, tile=tile)
        x = _ffn_call(x, mod, *ffn2, lng, lnb, layer=l, sub=2, alpha=alpha, tm=tm, f_chunk=f_chunk)
    return x
```
